```python
import math
import jax, jax.numpy as jnp
from jax import lax
import numpy as np

D_MODEL = 2048
BATCH = 2
SEQ = 4096
DEPTH = 2
DEC_BATCH = 32
DEC_SEQ = 8
PAST_LEN = 8192
PAGE_SIZE = 128

N_HGRN_LAYERS = (DEPTH + 1) // 2
N_ATTN_LAYERS = DEPTH // 2
HG_HEADS = 16
HG_KDIM = 128
HG_VDIM = D_MODEL // HG_HEADS
HG_CHUNK = 64
DA_HEAD_DIM = 128
DA_HEADS = D_MODEL // (2 * DA_HEAD_DIM)
DA_QK = 2 * DA_HEADS * DA_HEAD_DIM
DA_V = DA_HEADS * 2 * DA_HEAD_DIM
Q_BLOCK = 128
ROPE_THETA = 10000.0
D_FF = 7 * D_MODEL // 2
N_EXPERTS = 8
TOP_K = 2
NORM_EPS = 1e-6
SUBLN_EPS = 1e-5

kernel_name = "hgrn2_diffattn_moe_hybrid_step"


def _rmsnorm(x, g, eps=NORM_EPS):
    xf = x.astype(jnp.float32)
    y = xf * lax.rsqrt(jnp.mean(xf * xf, axis=-1, keepdims=True) + eps)
    return (y * g.astype(jnp.float32)).astype(x.dtype)


def _rope(x, pos):
    half = x.shape[-1] // 2
    inv = ROPE_THETA ** (-jnp.arange(half, dtype=jnp.float32) / half)
    ang = pos.astype(jnp.float32)[:, None] * inv[None, :]
    cos = jnp.cos(ang)[None, :, None, :]
    sin = jnp.sin(ang)[None, :, None, :]
    xf = x.astype(jnp.float32)
    x1, x2 = xf[..., :half], xf[..., half:]
    return jnp.concatenate([x1 * cos - x2 * sin, x2 * cos + x1 * sin], axis=-1).astype(x.dtype)


def _swiglu(h, w_gu, w_down):
    g, u = jnp.split(h @ w_gu, 2, axis=-1)
    return (jax.nn.silu(g) * u) @ w_down


def _moe(h, w_router, w_gu, w_down):
    shp = h.shape
    t = h.reshape(-1, shp[-1])
    logits = (t @ w_router).astype(jnp.float32)
    top_v, top_i = lax.top_k(logits, TOP_K)
    top_w = jax.nn.softmax(top_v, axis=-1)
    gates = jnp.sum(jax.nn.one_hot(top_i, N_EXPERTS, dtype=jnp.float32) * top_w[..., None], axis=1)
    out = jnp.zeros(t.shape, jnp.float32)
    for e in range(N_EXPERTS):
        out = out + gates[:, e:e + 1] * _swiglu(t, w_gu[e], w_down[e]).astype(jnp.float32)
    return out.astype(h.dtype).reshape(shp)


def _hgrn2_chunk(s0, q, k, lg, v):
    c = q.shape[2]
    G = jnp.cumsum(lg, axis=2)
    causal = jnp.tril(jnp.ones((c, c), dtype=bool))
    rel = jnp.where(causal[:, :, None], G[:, :, :, None, :] - G[:, :, None, :, :], -jnp.inf)
    a = jnp.einsum('bhtk,bhtsk,bhsk->bhts', q, jnp.exp(rel), k)
    o = jnp.einsum('bhtk,bhkv->bhtv', q * jnp.exp(G), s0) + jnp.einsum('bhts,bhsv->bhtv', a, v)
    g_last = G[:, :, -1:, :]
    s_new = jnp.exp(g_last[:, :, 0, :])[..., None] * s0 + jnp.einsum('bhsk,bhsv->bhkv', k * jnp.exp(g_last - G), v)
    return o, s_new


def _hgrn2_recurrence(s0, q, k, lg, v):
    b, h, t, _ = q.shape
    c = HG_CHUNK if t % HG_CHUNK == 0 else t
    n = t // c

    def blocks(a):
        return a.reshape(b, h, n, c, a.shape[-1]).transpose(2, 0, 1, 3, 4)

    def step(s, xs):
        o, s_next = _hgrn2_chunk(s, *xs)
        return s_next, o

    s_fin, o = lax.scan(step, s0, (blocks(q), blocks(k), blocks(lg), blocks(v)))
    o = o.transpose(1, 2, 0, 3, 4).reshape(b, h, t, -1)
    return o, s_fin


def _hgrn2_mixer(h, s0, w_in, lb, norm_g, w_out):
    b, t, _ = h.shape
    hk = HG_HEADS * HG_KDIM
    hv = HG_HEADS * HG_VDIM
    proj = h @ w_in
    q_pre, f_pre, i_in, g_out = jnp.split(proj, [hk, 2 * hk, 2 * hk + hv], axis=-1)
    q = jax.nn.silu(q_pre.astype(jnp.float32)) * HG_KDIM ** -0.5
    f = lb + (1.0 - lb) * jax.nn.sigmoid(f_pre.astype(jnp.float32))
    lg = jnp.log(f)
    k = 1.0 - f

    def heads(a, d):
        return a.reshape(b, t, HG_HEADS, d).transpose(0, 2, 1, 3)

    o, s_new = _hgrn2_recurrence(s0, heads(q, HG_KDIM), heads(k, HG_KDIM), heads(lg, HG_KDIM),
                                 heads(i_in.astype(jnp.float32), HG_VDIM))
    o = o.transpose(0, 2, 1, 3)
    o = _rmsnorm(o, norm_g) * jax.nn.silu(g_out.astype(jnp.float32)).reshape(b, t, HG_HEADS, HG_VDIM)
    return o.reshape(b, t, hv).astype(h.dtype) @ w_out, s_new.astype(h.dtype)


def _diff_lambda(lam_p, lambda_init):
    lp = lam_p.astype(jnp.float32)
    return jnp.exp(jnp.sum(lp[0] * lp[1])) - jnp.exp(jnp.sum(lp[2] * lp[3])) + lambda_init


def _diffattn_project(h, pos, w_in):
    b, t, _ = h.shape
    q, k, v = jnp.split(h @ w_in, [DA_QK, 2 * DA_QK], axis=-1)
    q = _rope(q.reshape(b, t, 2 * DA_HEADS, DA_HEAD_DIM), pos) * DA_HEAD_DIM ** -0.5
    k = _rope(k.reshape(b, t, 2 * DA_HEADS, DA_HEAD_DIM), pos)
    v = v.reshape(b, t, DA_HEADS, 2 * DA_HEAD_DIM)
    return q, k, v


def _diffattn_prompt(q, k, v, lam):
    b, s = q.shape[0], q.shape[1]
    nb = s // Q_BLOCK
    qf = q.astype(jnp.float32).reshape(b, nb, Q_BLOCK, 2 * DA_HEADS, DA_HEAD_DIM).transpose(1, 0, 2, 3, 4)
    kf = k.astype(jnp.float32)
    vf = v.astype(jnp.float32)
    kpos = jnp.arange(s, dtype=jnp.int32)

    def block(args):
        qblk, start = args
        sc = jnp.einsum('bqhd,bkhd->bhqk', qblk, kf)
        qpos = start + jnp.arange(Q_BLOCK, dtype=jnp.int32)
        sc = jnp.where((kpos[None, :] <= qpos[:, None])[None, None], sc, -jnp.inf)
        p = jax.nn.softmax(sc, axis=-1).reshape(b, DA_HEADS, 2, Q_BLOCK, s)
        w = p[:, :, 0] - lam * p[:, :, 1]
        return jnp.einsum('bhqk,bkhv->bqhv', w, vf)

    out = lax.map(block, (qf, jnp.arange(nb, dtype=jnp.int32) * Q_BLOCK))
    return out.transpose(1, 0, 2, 3, 4).reshape(b, s, DA_HEADS, 2 * DA_HEAD_DIM)


def _online_update(carry, s, v):
    m, l, acc = carry
    bsz, nh2, t, p = s.shape
    m_new = jnp.maximum(m, jnp.max(s, axis=-1))
    alpha = jnp.exp(m - m_new)
    pr = jnp.exp(s - m_new[..., None])
    l_new = alpha * l + jnp.sum(pr, axis=-1)
    pv = jnp.einsum('bgctp,bpgv->bgctv', pr.reshape(bsz, nh2 // 2, 2, t, p), v.astype(jnp.float32))
    acc_new = alpha[..., None] * acc + pv.reshape(bsz, nh2, t, -1)
    return m_new, l_new, acc_new


def _diffattn_sample(q, k_new, v_new, cache_k, cache_v, layer, page_table, lam):
    bd, t = q.shape[0], q.shape[1]
    qf = q.astype(jnp.float32)
    init = (jnp.full((bd, 2 * DA_HEADS, t), -jnp.inf, jnp.float32),
            jnp.zeros((bd, 2 * DA_HEADS, t), jnp.float32),
            jnp.zeros((bd, 2 * DA_HEADS, t, 2 * DA_HEAD_DIM), jnp.float32))

    def step(carry, pages):
        kp = cache_k[layer, pages].astype(jnp.float32)
        vp = cache_v[layer, pages]
        sc = jnp.einsum('bthd,bphd->bhtp', qf, kp)
        return _online_update(carry, sc, vp), None

    carry, _ = lax.scan(step, init, page_table.T)
    sc = jnp.einsum('bthd,bshd->bhts', qf, k_new.astype(jnp.float32))
    causal = jnp.tril(jnp.ones((t, t), dtype=bool))
    sc = jnp.where(causal[None, None], sc, -jnp.inf)
    m, l, acc = _online_update(carry, sc, v_new)
    attn = (acc / l[..., None]).reshape(bd, DA_HEADS, 2, t, 2 * DA_HEAD_DIM)
    out = attn[:, :, 0] - lam * attn[:, :, 1]
    return out.transpose(0, 2, 1, 3)


def _diffattn_out(o, subln_g, lambda_init, w_out, dtype):
    b, t = o.shape[0], o.shape[1]
    o = _rmsnorm(o, subln_g, SUBLN_EPS) * (1.0 - lambda_init)
    return o.reshape(b, t, DA_V).astype(dtype) @ w_out


def setup_inputs(seed: int = 0) -> dict:
    key = jax.random.key(seed)
    ks = jax.random.split(key, 24)
    n_pages = PAST_LEN // PAGE_SIZE
    n_used = DEC_BATCH * n_pages
    n_phys = n_used + n_used // 4
    hk = HG_HEADS * HG_KDIM
    hv = HG_HEADS * HG_VDIM

    def nrm(k, shape, scale):
        return jax.random.normal(k, shape, jnp.float32) * scale

    return {
        "x_prompt": nrm(ks[0], (BATCH, SEQ, D_MODEL), 1.0),
        "x_sample": nrm(ks[1], (DEC_BATCH, DEC_SEQ, D_MODEL), 1.0),
        "cache_k": nrm(ks[2], (N_ATTN_LAYERS, n_phys, PAGE_SIZE, 2 * DA_HEADS, DA_HEAD_DIM), 1.0),
        "cache_v": nrm(ks[3], (N_ATTN_LAYERS, n_phys, PAGE_SIZE, DA_HEADS, 2 * DA_HEAD_DIM), 1.0),
        "state_hgrn": nrm(ks[4], (N_HGRN_LAYERS, DEC_BATCH, HG_HEADS, HG_KDIM, HG_VDIM), 0.5),
        "page_table": jax.random.permutation(ks[5], n_phys)[:n_used].reshape(DEC_BATCH, n_pages).astype(jnp.int32),
        "norm_mix_g": 1.0 + nrm(ks[6], (DEPTH, D_MODEL), 0.02),
        "norm_ffn_g": 1.0 + nrm(ks[7], (DEPTH, D_MODEL), 0.02),
        "norm_out_g": 1.0 + nrm(ks[8], (D_MODEL,), 0.02),
        "hg_w_in": nrm(ks[9], (N_HGRN_LAYERS, D_MODEL, 2 * hk + 2 * hv), D_MODEL ** -0.5),
        "hg_lb_logits": nrm(ks[10], (N_HGRN_LAYERS + 1, hk), 0.5),
        "hg_norm_g": 1.0 + nrm(ks[11], (N_HGRN_LAYERS, HG_VDIM), 0.02),
        "hg_w_out": nrm(ks[12], (N_HGRN_LAYERS, hv, D_MODEL), hv ** -0.5),
        "da_w_in": nrm(ks[13], (N_ATTN_LAYERS, D_MODEL, 2 * DA_QK + DA_V), D_MODEL ** -0.5),
        "da_lambda": nrm(ks[14], (N_ATTN_LAYERS, 4, DA_HEAD_DIM), 0.1),
        "da_subln_g": 1.0 + nrm(ks[15], (N_ATTN_LAYERS, 2 * DA_HEAD_DIM), 0.02),
        "da_w_out": nrm(ks[16], (N_ATTN_LAYERS, DA_V, D_MODEL), DA_V ** -0.5),
        "ffn_w_gu": nrm(ks[17], (N_HGRN_LAYERS, D_MODEL, 2 * D_FF), D_MODEL ** -0.5),
        "ffn_w_down": nrm(ks[18], (N_HGRN_LAYERS, D_FF, D_MODEL), D_FF ** -0.5),
        "moe_w_router": nrm(ks[19], (N_ATTN_LAYERS, D_MODEL, N_EXPERTS), D_MODEL ** -0.5),
        "moe_w_gu": nrm(ks[20], (N_ATTN_LAYERS, N_EXPERTS, D_MODEL, 2 * D_FF), D_MODEL ** -0.5),
        "moe_w_down": nrm(ks[21], (N_ATTN_LAYERS, N_EXPERTS, D_FF, D_MODEL), D_FF ** -0.5),
    }


def reference(x_prompt, x_sample, cache_k, cache_v, state_hgrn, page_table,
              norm_mix_g, norm_ffn_g, norm_out_g,
              hg_w_in, hg_lb_logits, hg_norm_g, hg_w_out,
              da_w_in, da_lambda, da_subln_g, da_w_out,
              ffn_w_gu, ffn_w_down, moe_w_router, moe_w_gu, moe_w_down):
    pos_p = jnp.arange(SEQ, dtype=jnp.int32)
    pos_s = PAST_LEN + jnp.arange(DEC_SEQ, dtype=jnp.int32)
    lower_bounds = jnp.cumsum(jax.nn.softmax(hg_lb_logits.astype(jnp.float32), axis=0), axis=0)
    yp, ys = x_prompt, x_sample
    k_p, v_p, k_s, v_s, st_p, st_s = [], [], [], [], [], []
    for i in range(DEPTH):
        j = i // 2
        hp = _rmsnorm(yp, norm_mix_g[i])
        hs = _rmsnorm(ys, norm_mix_g[i])
        if i % 2 == 0:
            s0p = jnp.zeros((x_prompt.shape[0], HG_HEADS, HG_KDIM, HG_VDIM), jnp.float32)
            mp, sp = _hgrn2_mixer(hp, s0p, hg_w_in[j], lower_bounds[j], hg_norm_g[j], hg_w_out[j])
            ms, ss = _hgrn2_mixer(hs, state_hgrn[j].astype(jnp.float32), hg_w_in[j], lower_bounds[j],
                                  hg_norm_g[j], hg_w_out[j])
            st_p.append(sp)
            st_s.append(ss)
            yp = yp + mp
            ys = ys + ms
            yp = yp + _swiglu(_rmsnorm(yp, norm_ffn_g[i]), ffn_w_gu[j], ffn_w_down[j])
            ys = ys + _swiglu(_rmsnorm(ys, norm_ffn_g[i]), ffn_w_gu[j], ffn_w_down[j])
        else:
            lam_init = 0.8 - 0.6 * math.exp(-0.3 * i)
            lam = _diff_lambda(da_lambda[j], lam_init)
            qp, kp_, vp_ = _diffattn_project(hp, pos_p, da_w_in[j])
            qs, ks_, vs_ = _diffattn_project(hs, pos_s, da_w_in[j])
            op = _diffattn_prompt(qp, kp_, vp_, lam)
            os_ = _diffattn_sample(qs, ks_, vs_, cache_k, cache_v, j, page_table, lam)
            k_p.append(kp_)
            v_p.append(vp_)
            k_s.append(ks_)
            v_s.append(vs_)
            yp = yp + _diffattn_out(op, da_subln_g[j], lam_init, da_w_out[j], yp.dtype)
            ys = ys + _diffattn_out(os_, da_subln_g[j], lam_init, da_w_out[j], ys.dtype)
            yp = yp + _moe(_rmsnorm(yp, norm_ffn_g[i]), moe_w_router[j], moe_w_gu[j], moe_w_down[j])
            ys = ys + _moe(_rmsnorm(ys, norm_ffn_g[i]), moe_w_router[j], moe_w_gu[j], moe_w_down[j])
    y_prompt = _rmsnorm(yp, norm_out_g)
    y_sample = _rmsnorm(ys, norm_out_g)
    return (y_prompt, y_sample, jnp.stack(k_p), jnp.stack(v_p), jnp.stack(k_s), jnp.stack(v_s),
            jnp.stack(st_p), jnp.stack(st_s))
```

```python
import functools
import math

import numpy as np
import jax
import jax.numpy as jnp
from jax import lax
from jax.experimental import pallas as pl
from jax.experimental.pallas import tpu as pltpu

F32 = jnp.float32
BF16 = jnp.bfloat16

LANES = 128
NORM_EPS = 1e-6
SUBLN_EPS = 1e-5
ROPE_THETA = 10000.0
N_EXPERTS = 8
NEG_BIG = -1e30
VMEM_LIMIT = 56 * 1024 * 1024


def _params(**kw):
    return pltpu.CompilerParams(vmem_limit_bytes=VMEM_LIMIT, **kw)


def _dot(a, b):
    return jnp.dot(a, b, preferred_element_type=F32)


def _dot_nt(a, b):
    return lax.dot_general(a, b, (((1,), (1,)), ((), ())), preferred_element_type=F32)


def _dot_tn(a, b):
    return lax.dot_general(a, b, (((0,), (0,)), ((), ())), preferred_element_type=F32)


def _sigmoid(x):
    return 1.0 / (1.0 + jnp.exp(-x))


def _silu(x):
    return x * _sigmoid(x)


def _rms(x, g, eps):
    return x * lax.rsqrt(jnp.mean(x * x, axis=-1, keepdims=True) + eps) * g


def _split3(x):
    hi = x.astype(BF16)
    r1 = x - hi.astype(F32)
    mid = r1.astype(BF16)
    lo = (r1 - mid.astype(F32)).astype(BF16)
    return hi, mid, lo


def _norm_matmul_kernel(x_ref, g_ref, w_ref, o_ref, h_ref, *, eps):
    @pl.when(pl.program_id(1) == 0)
    def _():
        h_ref[...] = _rms(x_ref[...], g_ref[...], eps).astype(BF16)

    o_ref[...] = _dot(h_ref[...], w_ref[...].astype(BF16)).astype(o_ref.dtype)


def _norm_matmul(x, g, w, col0, ncols, *, eps, bm, bn, out_dtype=F32):
    t, d = x.shape
    cb = col0 // bn
    return pl.pallas_call(
        functools.partial(_norm_matmul_kernel, eps=eps),
        out_shape=jax.ShapeDtypeStruct((t, ncols), out_dtype),
        grid=(t // bm, ncols // bn),
        in_specs=[
            pl.BlockSpec((bm, d), lambda i, j: (i, 0)),
            pl.BlockSpec((1, d), lambda i, j: (0, 0)),
            pl.BlockSpec((d, bn), lambda i, j: (0, cb + j)),
        ],
        out_specs=pl.BlockSpec((bm, bn), lambda i, j: (i, j)),
        scratch_shapes=[pltpu.VMEM((bm, d), BF16)],
        compiler_params=_params(),
    )(x, g.reshape(1, d), w)


def _norm_rope_kernel(x_ref, g_ref, w_ref, cos_ref, sin_ref, o32_ref, o16_ref, h_ref, *, eps, scale, heads_per_blk):
    @pl.when(pl.program_id(1) == 0)
    def _():
        h_ref[...] = _rms(x_ref[...], g_ref[...], eps).astype(BF16)

    acc = _dot(h_ref[...], w_ref[...].astype(BF16))
    cos = cos_ref[...]
    sin = sin_ref[...]
    for c in range(heads_per_blk):
        sl = slice(c * LANES, (c + 1) * LANES)
        xh = acc[:, sl]
        r = (xh * cos + pltpu.roll(xh, LANES // 2, 1) * sin) * scale
        o32_ref[:, sl] = r
        o16_ref[:, sl] = r.astype(BF16)


def _norm_rope(x, g, w, col0, ncols, cos, sin, *, eps, scale, bm, bn):
    t, d = x.shape
    cb = col0 // bn
    return pl.pallas_call(
        functools.partial(_norm_rope_kernel, eps=eps, scale=scale, heads_per_blk=bn // LANES),
        out_shape=(jax.ShapeDtypeStruct((t, ncols), F32), jax.ShapeDtypeStruct((t, ncols), BF16)),
        grid=(t // bm, ncols // bn),
        in_specs=[
            pl.BlockSpec((bm, d), lambda i, j: (i, 0)),
            pl.BlockSpec((1, d), lambda i, j: (0, 0)),
            pl.BlockSpec((d, bn), lambda i, j: (0, cb + j)),
            pl.BlockSpec((bm, LANES), lambda i, j: (i, 0)),
            pl.BlockSpec((bm, LANES), lambda i, j: (i, 0)),
        ],
        out_specs=(pl.BlockSpec((bm, bn), lambda i, j: (i, j)), pl.BlockSpec((bm, bn), lambda i, j: (i, j))),
        scratch_shapes=[pltpu.VMEM((bm, d), BF16)],
        compiler_params=_params(),
    )(x, g.reshape(1, d), w, cos, sin)


def _matmul_residual_kernel(a_ref, w_ref, r_ref, o_ref):
    o_ref[...] = r_ref[...] + _dot(a_ref[...], w_ref[...].astype(BF16))


def _matmul_residual(a, w, res, *, bm, bn):
    t, k = a.shape
    n = w.shape[1]
    return pl.pallas_call(
        _matmul_residual_kernel,
        out_shape=jax.ShapeDtypeStruct((t, n), F32),
        grid=(t // bm, n // bn),
        in_specs=[
            pl.BlockSpec((bm, k), lambda i, j: (i, 0)),
            pl.BlockSpec((k, bn), lambda i, j: (0, j)),
            pl.BlockSpec((bm, bn), lambda i, j: (i, j)),
        ],
        out_specs=pl.BlockSpec((bm, bn), lambda i, j: (i, j)),
        compiler_params=_params(),
    )(a, w, res)


def _ffn_kernel(x_ref, g_ref, wg_ref, wu_ref, wd_ref, o_ref, h_ref, *, eps):
    @pl.when(pl.program_id(1) == 0)
    def _():
        x = x_ref[...]
        h_ref[...] = _rms(x, g_ref[...], eps).astype(BF16)
        o_ref[...] = x

    h = h_ref[...]
    gate = _dot(h, wg_ref[...].astype(BF16))
    up = _dot(h, wu_ref[...].astype(BF16))
    act = (_silu(gate) * up).astype(BF16)
    o_ref[...] += _dot(act, wd_ref[...].astype(BF16))


def _ffn_dense(y, g, w_gu, w_down, *, eps, bm, bf):
    t, d = y.shape
    dff = w_down.shape[0]
    nf = dff // bf
    return pl.pallas_call(
        functools.partial(_ffn_kernel, eps=eps),
        out_shape=jax.ShapeDtypeStruct((t, d), F32),
        grid=(t // bm, nf),
        in_specs=[
            pl.BlockSpec((bm, d), lambda i, f: (i, 0)),
            pl.BlockSpec((1, d), lambda i, f: (0, 0)),
            pl.BlockSpec((d, bf), lambda i, f: (0, f)),
            pl.BlockSpec((d, bf), lambda i, f: (0, nf + f)),
            pl.BlockSpec((bf, d), lambda i, f: (f, 0)),
        ],
        out_specs=pl.BlockSpec((bm, d), lambda i, f: (i, 0)),
        scratch_shapes=[pltpu.VMEM((bm, d), BF16)],
        compiler_params=_params(),
    )(y, g.reshape(1, d), w_gu, w_gu, w_down)


def _hgrn_consts(c, seg):
    t = np.arange(c)[:, None]
    p = np.arange(c)[None, :]
    same = (t // seg) == (p // seg)
    levels = []
    m = seg // 2
    while m >= 1:
        levels.append(m)
        m //= 2
    blocks = [(p <= t) & same, (p > t) & same]
    masks = []
    for m in levels:
        r = (t // (2 * m)) * (2 * m) + m - 1
        upper = (t % (2 * m)) >= m
        blocks.append(np.where(upper, (p > r) & (p <= t), (p > t) & (p <= r)))
        masks.append(((t // (2 * m)) == (p // (2 * m))) & upper & ((p % (2 * m)) < m))
    masks.append(t == p)
    w = np.concatenate(blocks, axis=0).astype(np.float32)
    return jnp.asarray(w, BF16), jnp.asarray(np.stack(masks).astype(np.float32)), len(levels)


def _hgrn_lower_bound(logits, layer):
    e = jnp.exp(logits - jnp.max(logits, axis=0, keepdims=True))
    sm = e / jnp.sum(e, axis=0, keepdims=True)
    return jnp.sum(sm[: layer + 1], axis=0, keepdims=True)


def _hgrn_gates(qp, fp, lb, kdim):
    q = _silu(qp) * (kdim ** -0.5)
    f = lb + (1.0 - lb) * _sigmoid(fp)
    return q, jnp.log(f), 1.0 - f


def _hgrn_intra(q, k, lg, w, m_ref, c, nlev):
    hi, mid, lo = _split3(lg)
    x = jnp.exp(_dot(w, hi) + _dot(w, mid) + _dot(w, lo))
    a = m_ref[nlev] * _dot_nt(q.astype(BF16), k.astype(BF16))
    for li in range(nlev):
        xm = x[(2 + li) * c:(3 + li) * c]
        a = a + m_ref[li] * _dot_nt((q * xm).astype(BF16), (k * xm).astype(BF16))
    return x, a


def _hgrn_prompt_kernel(q_ref, f_ref, i_ref, g_ref, lb_ref, ng_ref, w_ref, m_ref, o_ref, s_ref, st_ref,
                        *, hb, c, nlev, layer, eps):
    ci = pl.program_id(2)

    @pl.when(ci == 0)
    def _():
        st_ref[...] = jnp.zeros_like(st_ref)

    w = w_ref[...]
    for hh in range(hb):
        sl = slice(hh * LANES, (hh + 1) * LANES)
        lb = _hgrn_lower_bound(lb_ref[:, sl], layer)
        q, lg, k = _hgrn_gates(q_ref[:, sl], f_ref[:, sl], lb, LANES)
        x, a = _hgrn_intra(q, k, lg, w, m_ref, c, nlev)
        vb = i_ref[:, sl].astype(BF16)
        st = st_ref[hh]
        o = _dot_nt((q * x[0:c]).astype(BF16), st.astype(BF16)) + _dot(a.astype(BF16), vb)
        kl = (k * x[c:2 * c]).astype(BF16)
        st_ref[hh] = st * x[c - 1:c] + _dot_tn(vb, kl)
        o_ref[:, sl] = (_rms(o, ng_ref[...], eps) * _silu(g_ref[:, sl])).astype(o_ref.dtype)

    @pl.when(ci == pl.num_programs(2) - 1)
    def _():
        for hh in range(hb):
            s_ref[0, hh] = st_ref[hh].T


def _hgrn_prompt(proj, lb_logits, norm_g, n_seq, seq_len, *, layer, c, hb):
    hk = proj.shape[1] // 4
    nh = hk // LANES
    nc = seq_len // c
    ngrp = nh // hb
    w, masks, nlev = _hgrn_consts(c, c)
    bw = hb * LANES

    def col(part):
        return pl.BlockSpec((c, bw), lambda b, h, ci: (b * nc + ci, part * ngrp + h))

    return pl.pallas_call(
        functools.partial(_hgrn_prompt_kernel, hb=hb, c=c, nlev=nlev, layer=layer, eps=NORM_EPS),
        out_shape=(jax.ShapeDtypeStruct((n_seq * seq_len, hk), BF16),
                   jax.ShapeDtypeStruct((n_seq, nh, LANES, LANES), F32)),
        grid=(n_seq, ngrp, nc),
        in_specs=[
            col(0), col(1), col(2), col(3),
            pl.BlockSpec((lb_logits.shape[0], bw), lambda b, h, ci: (0, h)),
            pl.BlockSpec((1, LANES), lambda b, h, ci: (0, 0)),
            pl.BlockSpec(w.shape, lambda b, h, ci: (0, 0)),
            pl.BlockSpec(masks.shape, lambda b, h, ci: (0, 0, 0)),
        ],
        out_specs=(pl.BlockSpec((c, bw), lambda b, h, ci: (b * nc + ci, h)),
                   pl.BlockSpec((1, hb, LANES, LANES), lambda b, h, ci: (b, h, 0, 0))),
        scratch_shapes=[pltpu.VMEM((hb, LANES, LANES), F32)],
        compiler_params=_params(),
    )(proj, proj, proj, proj, lb_logits, norm_g.reshape(1, LANES), w, masks)


def _hgrn_sample_kernel(q_ref, f_ref, i_ref, g_ref, lb_ref, ng_ref, w_ref, m_ref, s0_ref, o_ref, s_ref,
                        *, hb, nb, tq, nlev, layer, eps):
    c = nb * tq
    w = w_ref[...]
    seq_of_row = lax.broadcasted_iota(jnp.int32, (c, 1), 0) // tq
    for hh in range(hb):
        sl = slice(hh * LANES, (hh + 1) * LANES)
        lb = _hgrn_lower_bound(lb_ref[:, sl], layer)
        q, lg, k = _hgrn_gates(q_ref[:, sl], f_ref[:, sl], lb, LANES)
        x, a = _hgrn_intra(q, k, lg, w, m_ref, c, nlev)
        vb = i_ref[:, sl].astype(BF16)
        q0 = (q * x[0:c]).astype(BF16)
        kl = k * x[c:2 * c]
        o = _dot(a.astype(BF16), vb)
        for b in range(nb):
            mine = seq_of_row == b
            st = s0_ref[b, hh].T
            o = o + jnp.where(mine, _dot_nt(q0, st.astype(BF16)), 0.0)
            r_last = (b + 1) * tq - 1
            klb = jnp.where(mine, kl, 0.0).astype(BF16)
            s_ref[b, hh] = (st * x[r_last:r_last + 1] + _dot_tn(vb, klb)).T
        o_ref[:, sl] = (_rms(o, ng_ref[...], eps) * _silu(g_ref[:, sl])).astype(o_ref.dtype)


def _hgrn_sample(proj, row0, state, lb_logits, norm_g, tq, *, layer, nb, hb):
    n_seq, nh = state.shape[0], state.shape[1]
    hk = proj.shape[1] // 4
    ngrp = nh // hb
    c = nb * tq
    rb0 = row0 // c
    w, masks, nlev = _hgrn_consts(c, tq)
    bw = hb * LANES

    def col(part):
        return pl.BlockSpec((c, bw), lambda sb, h: (rb0 + sb, part * ngrp + h))

    return pl.pallas_call(
        functools.partial(_hgrn_sample_kernel, hb=hb, nb=nb, tq=tq, nlev=nlev, layer=layer, eps=NORM_EPS),
        out_shape=(jax.ShapeDtypeStruct((n_seq * tq, hk), BF16),
                   jax.ShapeDtypeStruct(state.shape, F32)),
        grid=(n_seq // nb, ngrp),
        in_specs=[
            col(0), col(1), col(2), col(3),
            pl.BlockSpec((lb_logits.shape[0], bw), lambda sb, h: (0, h)),
            pl.BlockSpec((1, LANES), lambda sb, h: (0, 0)),
            pl.BlockSpec(w.shape, lambda sb, h: (0, 0)),
            pl.BlockSpec(masks.shape, lambda sb, h: (0, 0, 0)),
            pl.BlockSpec((nb, hb, LANES, LANES), lambda sb, h: (sb, h, 0, 0)),
        ],
        out_specs=(pl.BlockSpec((c, bw), lambda sb, h: (sb, h)),
                   pl.BlockSpec((nb, hb, LANES, LANES), lambda sb, h: (sb, h, 0, 0))),
        compiler_params=_params(),
    )(proj, proj, proj, proj, lb_logits, norm_g.reshape(1, LANES), w, masks, state)


def _diff_lambda(lam_ref, lam_init):
    lp = lam_ref[...]
    s1 = jnp.sum(lp[0:1] * lp[1:2], axis=-1, keepdims=True)
    s2 = jnp.sum(lp[2:3] * lp[3:4], axis=-1, keepdims=True)
    return jnp.exp(s1) - jnp.exp(s2) + lam_init


def _attn_prompt_kernel(lam_ref, sg_ref, q_ref, k_ref, v_ref, o_ref, m_ref, l_ref, acc_ref, *, blk, lam_init, eps):
    i = pl.program_id(2)
    j = pl.program_id(3)

    @pl.when(j == 0)
    def _():
        m_ref[...] = jnp.full_like(m_ref, NEG_BIG)
        l_ref[...] = jnp.zeros_like(l_ref)
        acc_ref[...] = jnp.zeros_like(acc_ref)

    def step(diagonal):
        vb = v_ref[...]
        for s in range(2):
            sl = slice(s * LANES, (s + 1) * LANES)
            sc = _dot_nt(q_ref[:, sl], k_ref[:, sl])
            if diagonal:
                row = lax.broadcasted_iota(jnp.int32, (blk, blk), 0)
                colv = lax.broadcasted_iota(jnp.int32, (blk, blk), 1)
                sc = jnp.where(colv <= row, sc, NEG_BIG)
            m_prev = m_ref[s]
            m_new = jnp.maximum(m_prev, jnp.max(sc, axis=-1, keepdims=True))
            alpha = jnp.exp(m_prev - m_new)
            p = jnp.exp(sc - m_new)
            l_ref[s] = alpha * l_ref[s] + jnp.sum(p, axis=-1, keepdims=True)
            acc_ref[s] = alpha * acc_ref[s] + _dot(p.astype(BF16), vb)
            m_ref[s] = m_new

    @pl.when(j < i)
    def _():
        step(False)

    @pl.when(j == i)
    def _():
        step(True)
        lam = _diff_lambda(lam_ref, lam_init)
        o = acc_ref[0] / l_ref[0] - lam * (acc_ref[1] / l_ref[1])
        o_ref[...] = (_rms(o, sg_ref[...], eps) * (1.0 - lam_init)).astype(o_ref.dtype)


def _attn_prompt(q, k, v, lam_p, subln_g, n_seq, seq_len, *, lam_init, blk):
    hv = 2 * LANES
    nh = q.shape[1] // hv
    nb = seq_len // blk
    return pl.pallas_call(
        functools.partial(_attn_prompt_kernel, blk=blk, lam_init=lam_init, eps=SUBLN_EPS),
        out_shape=jax.ShapeDtypeStruct((n_seq * seq_len, nh * hv), BF16),
        grid=(n_seq, nh, nb, nb),
        in_specs=[
            pl.BlockSpec(lam_p.shape, lambda b, h, i, j: (0, 0)),
            pl.BlockSpec((1, hv), lambda b, h, i, j: (0, 0)),
            pl.BlockSpec((blk, hv), lambda b, h, i, j: (b * nb + i, h)),
            pl.BlockSpec((blk, hv), lambda b, h, i, j: (b * nb + jnp.minimum(i, j), h)),
            pl.BlockSpec((blk, hv), lambda b, h, i, j: (b * nb + jnp.minimum(i, j), h)),
        ],
        out_specs=pl.BlockSpec((blk, hv), lambda b, h, i, j: (b * nb + i, h)),
        scratch_shapes=[pltpu.VMEM((2, blk, 1), F32), pltpu.VMEM((2, blk, 1), F32), pltpu.VMEM((2, blk, hv), F32)],
        compiler_params=_params(),
    )(lam_p, subln_g.reshape(1, hv), q, k, v)


def _attn_sample_kernel(pt_ref, lam_ref, sg_ref, q_ref, kn_ref, vn_ref, *rest, pages, tq, nh, lam_init, eps):
    k_refs = rest[:pages]
    v_refs = rest[pages:2 * pages]
    o_ref, m_ref, l_ref, acc_ref = rest[2 * pages:]
    g = pl.program_id(1)
    hv = 2 * LANES
    rows = 2 * tq

    @pl.when(g == 0)
    def _():
        m_ref[...] = jnp.full_like(m_ref, NEG_BIG)
        l_ref[...] = jnp.zeros_like(l_ref)
        acc_ref[...] = jnp.zeros_like(acc_ref)

    first = lax.broadcasted_iota(jnp.int32, (rows, 1), 0) < tq

    def scores(kb):
        out = []
        for h in range(nh):
            qa = q_ref[h * rows:(h + 1) * rows, :]
            sa = _dot_nt(qa, kb[:, (2 * h) * LANES:(2 * h + 1) * LANES])
            sb = _dot_nt(qa, kb[:, (2 * h + 1) * LANES:(2 * h + 2) * LANES])
            out.append(jnp.where(first, sa, sb))
        return jnp.concatenate(out, axis=0)

    def update(sc, vb):
        m_prev = m_ref[...]
        m_new = jnp.maximum(m_prev, jnp.max(sc, axis=-1, keepdims=True))
        alpha = jnp.exp(m_prev - m_new)
        p = jnp.exp(sc - m_new)
        l_ref[...] = alpha * l_ref[...] + jnp.sum(p, axis=-1, keepdims=True)
        pb = p.astype(BF16)
        pv = [_dot(pb[h * rows:(h + 1) * rows], vb[:, h * hv:(h + 1) * hv]) for h in range(nh)]
        acc_ref[...] = alpha * acc_ref[...] + jnp.concatenate(pv, axis=0)
        m_ref[...] = m_new

    for pg in range(pages):
        update(scores(k_refs[pg][...].astype(BF16)), v_refs[pg][...].astype(BF16))

    @pl.when(g == pl.num_programs(1) - 1)
    def _():
        pad = jnp.zeros((LANES - tq, kn_ref.shape[1]), F32)
        kb = jnp.concatenate([kn_ref[...], pad], axis=0).astype(BF16)
        vb = jnp.concatenate([vn_ref[...], pad], axis=0).astype(BF16)
        sc = scores(kb)
        qpos = lax.broadcasted_iota(jnp.int32, sc.shape, 0) % tq
        kpos = lax.broadcasted_iota(jnp.int32, sc.shape, 1)
        update(jnp.where(kpos <= qpos, sc, NEG_BIG), vb)
        lam = _diff_lambda(lam_ref, lam_init)
        attn = acc_ref[...] / l_ref[...]
        sg = sg_ref[...]
        for h in range(nh):
            o = attn[h * rows:h * rows + tq] - lam * attn[h * rows + tq:(h + 1) * rows]
            o_ref[:, h * hv:(h + 1) * hv] = _rms(o, sg, eps) * (1.0 - lam_init)


def _attn_sample(q16, k_new, v_new, row0, cache_k, cache_v, page_table, lam_p, subln_g, tq, *, lam_init, pages):
    n_seq, n_pages = page_table.shape
    width = cache_k.shape[2]
    psz = cache_k.shape[1]
    hv = 2 * LANES
    nh = width // hv
    rows = nh * 2 * tq
    rb0 = row0 // tq

    def page_spec(pg):
        return pl.BlockSpec((None, psz, width), lambda b, g, pt: (pt[b * n_pages + g * pages + pg], 0, 0))

    return pl.pallas_call(
        functools.partial(_attn_sample_kernel, pages=pages, tq=tq, nh=nh, lam_init=lam_init, eps=SUBLN_EPS),
        out_shape=jax.ShapeDtypeStruct((n_seq, tq, width), F32),
        grid_spec=pltpu.PrefetchScalarGridSpec(
            num_scalar_prefetch=1,
            grid=(n_seq, n_pages // pages),
            in_specs=[
                pl.BlockSpec(lam_p.shape, lambda b, g, pt: (0, 0)),
                pl.BlockSpec((1, hv), lambda b, g, pt: (0, 0)),
                pl.BlockSpec((None, rows, LANES), lambda b, g, pt: (b, 0, 0)),
                pl.BlockSpec((tq, width), lambda b, g, pt: (rb0 + b, 0)),
                pl.BlockSpec((tq, width), lambda b, g, pt: (rb0 + b, 0)),
            ] + [page_spec(pg) for pg in range(pages)] + [page_spec(pg) for pg in range(pages)],
            out_specs=pl.BlockSpec((None, tq, width), lambda b, g, pt: (b, 0, 0)),
            scratch_shapes=[pltpu.VMEM((rows, 1), F32), pltpu.VMEM((rows, 1), F32), pltpu.VMEM((rows, hv), F32)],
        ),
        compiler_params=_params(),
    )(page_table.reshape(-1), lam_p, subln_g.reshape(1, hv), q16, k_new, v_new,
      *([cache_k] * pages), *([cache_v] * pages))


def _moe_route_kernel(y_ref, g_ref, wr_ref, hn_ref, r_ref, *, eps, n_exp):
    h = _rms(y_ref[...], g_ref[...], eps)
    hn_ref[...] = h
    h1, h2, h3 = _split3(h)
    w1, w2, w3 = _split3(wr_ref[...])
    logits = (_dot(h1, w1) + (_dot(h1, w2) + _dot(h2, w1)) + (_dot(h1, w3) + _dot(h2, w2) + _dot(h3, w1)))
    lane = lax.broadcasted_iota(jnp.int32, logits.shape, 1)
    logits = jnp.where(lane < n_exp, logits, NEG_BIG)
    m1 = jnp.max(logits, axis=-1, keepdims=True)
    i1 = jnp.min(jnp.where(logits == m1, lane, LANES), axis=-1, keepdims=True)
    rest = jnp.where(lane == i1, NEG_BIG, logits)
    m2 = jnp.max(rest, axis=-1, keepdims=True)
    i2 = jnp.min(jnp.where(rest == m2, lane, LANES), axis=-1, keepdims=True)
    e2 = jnp.exp(m2 - m1)
    w_top1 = 1.0 / (1.0 + e2)
    w_top2 = e2 / (1.0 + e2)
    r_ref[...] = jnp.where(lane == 0, i1.astype(F32),
                           jnp.where(lane == 1, i2.astype(F32),
                                     jnp.where(lane == 2, w_top1, jnp.where(lane == 3, w_top2, 0.0))))


def _moe_route(y, g, w_router, *, bm):
    t, d = y.shape
    n_exp = w_router.shape[1]
    wr = jnp.pad(w_router, ((0, 0), (0, LANES - n_exp)))
    return pl.pallas_call(
        functools.partial(_moe_route_kernel, eps=NORM_EPS, n_exp=n_exp),
        out_shape=(jax.ShapeDtypeStruct((t, d), F32), jax.ShapeDtypeStruct((t, LANES), F32)),
        grid=(t // bm,),
        in_specs=[
            pl.BlockSpec((bm, d), lambda i: (i, 0)),
            pl.BlockSpec((1, d), lambda i: (0, 0)),
            pl.BlockSpec((d, LANES), lambda i: (0, 0)),
        ],
        out_specs=(pl.BlockSpec((bm, d), lambda i: (i, 0)), pl.BlockSpec((bm, LANES), lambda i: (i, 0))),
        compiler_params=_params(),
    )(y, g.reshape(1, d), wr)


def _row_copy(src_hbm, dst, src_row, dst_row, sem):
    return pltpu.make_async_copy(src_hbm.at[pl.ds(src_row, 1)], dst.at[pl.ds(dst_row, 1)], sem)


def _gather_rows_kernel(idx_ref, src_ref, dst_ref, sem, *, chunk):
    base = pl.program_id(0) * chunk

    def issue(r, carry):
        _row_copy(src_ref, dst_ref, idx_ref[0, 0, r], base + r, sem).start()
        return carry

    lax.fori_loop(0, chunk, issue, 0)

    def drain(r, carry):
        _row_copy(src_ref, dst_ref, 0, base + r, sem).wait()
        return carry

    lax.fori_loop(0, chunk, drain, 0)


def _gather_rows(src, idx, *, chunk):
    n = idx.shape[0]
    d = src.shape[1]
    steps = n // chunk
    return pl.pallas_call(
        functools.partial(_gather_rows_kernel, chunk=chunk),
        out_shape=jax.ShapeDtypeStruct((n, d), src.dtype),
        grid=(steps,),
        in_specs=[
            pl.BlockSpec((1, 1, chunk), lambda i: (i, 0, 0), memory_space=pltpu.SMEM),
            pl.BlockSpec(memory_space=pl.ANY),
        ],
        out_specs=pl.BlockSpec(memory_space=pl.ANY),
        scratch_shapes=[pltpu.SemaphoreType.DMA],
        compiler_params=_params(),
    )(idx.reshape(steps, 1, chunk), src)


def _moe_expert_kernel(te_ref, nu_ref, xs_ref, gate_ref, wg_ref, wu_ref, wd_ref, o_ref, xb_ref):
    i = pl.program_id(0)
    f = pl.program_id(1)
    used = i < nu_ref[0]

    @pl.when(f == 0)
    def _():
        xb_ref[...] = xs_ref[...].astype(BF16)
        o_ref[...] = jnp.zeros_like(o_ref)

    @pl.when(used)
    def _():
        x = xb_ref[...]
        gate = _dot(x, wg_ref[...].astype(BF16))
        up = _dot(x, wu_ref[...].astype(BF16))
        act = (_silu(gate) * up).astype(BF16)
        o_ref[...] += _dot(act, wd_ref[...].astype(BF16))

    @pl.when(used & (f == pl.num_programs(1) - 1))
    def _():
        o_ref[...] = o_ref[...] * gate_ref[...]


def _moe_experts(xs, gate_sorted, tile_expert, n_used, w_gu, w_down, *, bm, bf):
    p, d = xs.shape
    dff = w_down.shape[1]
    nf = dff // bf

    def fblk(i, f, nu):
        return jnp.where(i < nu[0], f, nf - 1)

    return pl.pallas_call(
        _moe_expert_kernel,
        out_shape=jax.ShapeDtypeStruct((p, d), F32),
        grid_spec=pltpu.PrefetchScalarGridSpec(
            num_scalar_prefetch=2,
            grid=(p // bm, nf),
            in_specs=[
                pl.BlockSpec((bm, d), lambda i, f, te, nu: (i, 0)),
                pl.BlockSpec((bm, 1), lambda i, f, te, nu: (i, 0)),
                pl.BlockSpec((None, d, bf), lambda i, f, te, nu: (te[i], 0, fblk(i, f, nu))),
                pl.BlockSpec((None, d, bf), lambda i, f, te, nu: (te[i], 0, nf + fblk(i, f, nu))),
                pl.BlockSpec((None, bf, d), lambda i, f, te, nu: (te[i], fblk(i, f, nu), 0)),
            ],
            out_specs=pl.BlockSpec((bm, d), lambda i, f, te, nu: (i, 0)),
            scratch_shapes=[pltpu.VMEM((bm, d), BF16)],
        ),
        compiler_params=_params(),
    )(tile_expert, n_used, xs, gate_sorted.reshape(p, 1), w_gu, w_gu, w_down)


def _moe_combine_kernel(pos_ref, y_ref, g_ref, ys_ref, o_ref, buf_ref, sem, *, bt, eps):
    def issue(r, carry):
        _row_copy(ys_ref, buf_ref.at[0], pos_ref[0, 0, 2 * r], r, sem).start()
        _row_copy(ys_ref, buf_ref.at[1], pos_ref[0, 0, 2 * r + 1], r, sem).start()
        return carry

    lax.fori_loop(0, bt, issue, 0)

    def drain(r, carry):
        _row_copy(ys_ref, buf_ref.at[0], 0, r, sem).wait()
        _row_copy(ys_ref, buf_ref.at[1], 0, r, sem).wait()
        return carry

    lax.fori_loop(0, bt, drain, 0)
    z = y_ref[...] + (buf_ref[0] + buf_ref[1])
    o_ref[...] = _rms(z, g_ref[...], eps)


def _moe_combine_norm(y, ys, pos, g, *, bt):
    t, d = y.shape
    steps = t // bt
    return pl.pallas_call(
        functools.partial(_moe_combine_kernel, bt=bt, eps=NORM_EPS),
        out_shape=jax.ShapeDtypeStruct((t, d), F32),
        grid=(steps,),
        in_specs=[
            pl.BlockSpec((1, 1, 2 * bt), lambda i: (i, 0, 0), memory_space=pltpu.SMEM),
            pl.BlockSpec((bt, d), lambda i: (i, 0)),
            pl.BlockSpec((1, d), lambda i: (0, 0)),
            pl.BlockSpec(memory_space=pl.ANY),
        ],
        out_specs=pl.BlockSpec((bt, d), lambda i: (i, 0)),
        scratch_shapes=[pltpu.VMEM((2, bt, d), F32), pltpu.SemaphoreType.DMA],
        compiler_params=_params(),
    )(pos.reshape(steps, 1, 2 * bt), y, g.reshape(1, d), ys)


def _moe_plan(route, n_exp, bm, n_tiles):
    t = route.shape[0]
    experts = route[:, 0:2].astype(jnp.int32).reshape(-1)
    gates = route[:, 2:4].reshape(-1)
    onehot = (experts[:, None] == jnp.arange(n_exp, dtype=jnp.int32)[None, :]).astype(jnp.int32)
    rank = jnp.sum((jnp.cumsum(onehot, axis=0) - onehot) * onehot, axis=1)
    counts = jnp.sum(onehot, axis=0)
    tiles_per = (counts + bm - 1) // bm
    tile_end = jnp.cumsum(tiles_per)
    row_start = (tile_end - tiles_per) * bm
    dest = row_start[experts] + rank
    p = n_tiles * bm
    token_sorted = jnp.zeros((p,), jnp.int32).at[dest].set(jnp.arange(2 * t, dtype=jnp.int32) // 2)
    gate_sorted = jnp.zeros((p,), F32).at[dest].set(gates)
    n_used = tile_end[-1]
    tile_ids = jnp.minimum(jnp.arange(n_tiles, dtype=jnp.int32), n_used - 1)
    tile_expert = jnp.sum((tile_ids[:, None] >= tile_end[None, :]).astype(jnp.int32), axis=1)
    return token_sorted, gate_sorted, tile_expert.astype(jnp.int32), n_used.reshape(1).astype(jnp.int32), dest


def _moe_block_norm(y, g_ffn, w_router, w_gu, w_down, g_out, *, bm_route, bm, bf, chunk, bt):
    t = y.shape[0]
    n_exp = w_router.shape[1]
    hn, route = _moe_route(y, g_ffn, w_router, bm=bm_route)
    n_tiles = (2 * t) // bm + n_exp
    token_sorted, gate_sorted, tile_expert, n_used, dest = _moe_plan(route, n_exp, bm, n_tiles)
    xs = _gather_rows(hn, token_sorted, chunk=chunk)
    ys = _moe_experts(xs, gate_sorted, tile_expert, n_used, w_gu, w_down, bm=bm, bf=bf)
    return _moe_combine_norm(y, ys, dest, g_out, bt=bt)


def _rope_tables(positions, half):
    inv = ROPE_THETA ** (-jnp.arange(half, dtype=F32) / half)
    ang = positions.astype(F32)[:, None] * inv[None, :]
    cos, sin = jnp.cos(ang), jnp.sin(ang)
    return jnp.concatenate([cos, cos], axis=1), jnp.concatenate([-sin, sin], axis=1)


def _step(x_prompt, x_sample, cache_k, cache_v, state_hgrn, page_table, norm_mix_g, norm_ffn_g, norm_out_g,
          hg_w_in, hg_lb_logits, hg_norm_g, hg_w_out, da_w_in, da_lambda, da_subln_g, da_w_out,
          ffn_w_gu, ffn_w_down, moe_w_router, moe_w_gu, moe_w_down, *, cfg):
    nb_p, seq, d = x_prompt.shape
    nb_s, tq, _ = x_sample.shape
    tp, ts = nb_p * seq, nb_s * tq
    past = page_table.shape[1] * cache_k.shape[2]
    dh = cache_k.shape[-1]
    n_sub = cache_k.shape[-2]
    nh_attn = n_sub // 2
    y = jnp.concatenate([x_prompt.reshape(tp, d), x_sample.reshape(ts, d)], axis=0)

    hk = hg_w_in.shape[2] // 4
    proj = _norm_matmul(y, norm_mix_g[0], hg_w_in[0], 0, 4 * hk, eps=NORM_EPS, bm=cfg["bm"], bn=cfg["bn"])
    o_p, st_p = _hgrn_prompt(proj, hg_lb_logits, hg_norm_g[0], nb_p, seq, layer=0, c=cfg["hg_chunk"], hb=cfg["hg_hb"])
    o_s, st_s = _hgrn_sample(proj, tp, state_hgrn[0], hg_lb_logits, hg_norm_g[0], tq, layer=0,
                             nb=cfg["hg_nb"], hb=cfg["hg_hb"])
    y = _matmul_residual(jnp.concatenate([o_p, o_s], axis=0), hg_w_out[0], y, bm=cfg["bm"], bn=cfg["bn"])
    y = _ffn_dense(y, norm_ffn_g[0], ffn_w_gu[0], ffn_w_down[0], eps=NORM_EPS, bm=cfg["ffn_bm"], bf=cfg["ffn_bf"])

    lam_init = 0.8 - 0.6 * math.exp(-0.3 * 1)
    pos = jnp.concatenate([jnp.tile(jnp.arange(seq, dtype=jnp.int32), nb_p),
                           jnp.tile(past + jnp.arange(tq, dtype=jnp.int32), nb_s)])
    cos, sin = _rope_tables(pos, dh // 2)
    qk = n_sub * dh
    g1 = norm_mix_g[1]
    w_in = da_w_in[0]
    _, q16 = _norm_rope(y, g1, w_in, 0, qk, cos, sin, eps=NORM_EPS, scale=dh ** -0.5, bm=cfg["bm"], bn=cfg["bn"])
    k32, k16 = _norm_rope(y, g1, w_in, qk, qk, cos, sin, eps=NORM_EPS, scale=1.0, bm=cfg["bm"], bn=cfg["bn"])
    v32 = _norm_matmul(y, g1, w_in, 2 * qk, qk, eps=NORM_EPS, bm=cfg["bm"], bn=cfg["bn"])
    v16 = v32.astype(BF16)
    a_p = _attn_prompt(q16, k16, v16, da_lambda[0], da_subln_g[0], nb_p, seq, lam_init=lam_init, blk=cfg["attn_blk"])
    q_s = q16[tp:].reshape(nb_s, tq, nh_attn, 2, dh).transpose(0, 2, 3, 1, 4).reshape(nb_s, nh_attn * 2 * tq, dh)
    ck = cache_k[0].reshape(cache_k.shape[1], cache_k.shape[2], qk)
    cv = cache_v[0].reshape(cache_v.shape[1], cache_v.shape[2], qk)
    a_s = _attn_sample(q_s, k32, v32, tp, ck, cv, page_table, da_lambda[0], da_subln_g[0], tq,
                       lam_init=lam_init, pages=cfg["attn_pages"])
    attn = jnp.concatenate([a_p, a_s.reshape(ts, qk).astype(BF16)], axis=0)
    y = _matmul_residual(attn, da_w_out[0], y, bm=cfg["bm"], bn=cfg["bn"])
    y = _moe_block_norm(y, norm_ffn_g[1], moe_w_router[0], moe_w_gu[0], moe_w_down[0], norm_out_g,
                        bm_route=cfg["route_bm"], bm=cfg["moe_bm"], bf=cfg["moe_bf"],
                        chunk=cfg["gather_chunk"], bt=cfg["combine_bt"])

    return (y[:tp].reshape(nb_p, seq, d), y[tp:].reshape(nb_s, tq, d),
            k32[:tp].reshape(1, nb_p, seq, n_sub, dh), v32[:tp].reshape(1, nb_p, seq, nh_attn, 2 * dh),
            k32[tp:].reshape(1, nb_s, tq, n_sub, dh), v32[tp:].reshape(1, nb_s, tq, nh_attn, 2 * dh),
            st_p[None], st_s[None])


_CFG = dict(bm=1056, bn=512, hg_chunk=128, hg_hb=4, hg_nb=8, ffn_bm=768, ffn_bf=256, attn_blk=512,
            attn_pages=4, route_bm=264, moe_bm=512, moe_bf=512, gather_chunk=512, combine_bt=264)


def kernel(x_prompt, x_sample, cache_k, cache_v, state_hgrn, page_table, norm_mix_g, norm_ffn_g, norm_out_g,
           hg_w_in, hg_lb_logits, hg_norm_g, hg_w_out, da_w_in, da_lambda, da_subln_g, da_w_out,
           ffn_w_gu, ffn_w_down, moe_w_router, moe_w_gu, moe_w_down):
    return _step(x_prompt, x_sample, cache_k, cache_v, state_hgrn, page_table, norm_mix_g, norm_ffn_g, norm_out_g,
                 hg_w_in, hg_lb_logits, hg_norm_g, hg_w_out, da_w_in, da_lambda, da_subln_g, da_w_out,
                 ffn_w_gu, ffn_w_down, moe_w_router, moe_w_gu, moe_w_down, cfg=_CFG)
```

```python
import functools
import math

import numpy as np
import jax
import jax.numpy as jnp
from jax import lax
from jax.experimental import pallas as pl
from jax.experimental.pallas import tpu as pltpu

F32 = jnp.float32
BF16 = jnp.bfloat16

LANES = 128
NORM_EPS = 1e-6
SUBLN_EPS = 1e-5
ROPE_THETA = 10000.0
N_EXPERTS = 8
NEG_BIG = -1e30
VMEM_LIMIT = 56 * 1024 * 1024


def _params(**kw):
    return pltpu.CompilerParams(vmem_limit_bytes=VMEM_LIMIT, **kw)


def _dot(a, b):
    return jnp.dot(a, b, preferred_element_type=F32)


def _dot_nt(a, b):
    return lax.dot_general(a, b, (((1,), (1,)), ((), ())), preferred_element_type=F32)


def _dot_tn(a, b):
    return lax.dot_general(a, b, (((0,), (0,)), ((), ())), preferred_element_type=F32)


def _sigmoid(x):
    return 1.0 / (1.0 + jnp.exp(-x))


def _silu(x):
    return x * _sigmoid(x)


def _rms(x, g, eps):
    return x * lax.rsqrt(jnp.mean(x * x, axis=-1, keepdims=True) + eps) * g


def _split3(x):
    hi = x.astype(BF16)
    r1 = x - hi.astype(F32)
    mid = r1.astype(BF16)
    lo = (r1 - mid.astype(F32)).astype(BF16)
    return hi, mid, lo


def _norm_matmul_kernel(x_ref, g_ref, w_ref, *rest, eps):
    o_refs, h_ref = rest[:-1], rest[-1]

    @pl.when(pl.program_id(1) == 0)
    def _():
        h_ref[...] = _rms(x_ref[...], g_ref[...], eps).astype(BF16)

    acc = _dot(h_ref[...], w_ref[...].astype(BF16))
    for o_ref in o_refs:
        o_ref[...] = acc.astype(o_ref.dtype)


def _norm_matmul(x, g, w, col0, ncols, *, eps, bm, bn, out_dtypes=(F32,)):
    t, d = x.shape
    cb = col0 // bn
    return pl.pallas_call(
        functools.partial(_norm_matmul_kernel, eps=eps),
        out_shape=tuple(jax.ShapeDtypeStruct((t, ncols), dt) for dt in out_dtypes),
        grid=(t // bm, ncols // bn),
        in_specs=[
            pl.BlockSpec((bm, d), lambda i, j: (i, 0)),
            pl.BlockSpec((1, d), lambda i, j: (0, 0)),
            pl.BlockSpec((d, bn), lambda i, j: (0, cb + j)),
        ],
        out_specs=tuple(pl.BlockSpec((bm, bn), lambda i, j: (i, j)) for _ in out_dtypes),
        scratch_shapes=[pltpu.VMEM((bm, d), BF16)],
        compiler_params=_params(),
    )(x, g.reshape(1, d), w)


def _norm_rope_kernel(x_ref, g_ref, w_ref, cos_ref, sin_ref, o32_ref, o16_ref, h_ref, *, eps, scale, heads_per_blk):
    @pl.when(pl.program_id(1) == 0)
    def _():
        h_ref[...] = _rms(x_ref[...], g_ref[...], eps).astype(BF16)

    acc = _dot(h_ref[...], w_ref[...].astype(BF16))
    cos = cos_ref[...]
    sin = sin_ref[...]
    for c in range(heads_per_blk):
        sl = slice(c * LANES, (c + 1) * LANES)
        xh = acc[:, sl]
        r = (xh * cos + pltpu.roll(xh, LANES // 2, 1) * sin) * scale
        o32_ref[:, sl] = r
        o16_ref[:, sl] = r.astype(BF16)


def _norm_rope(x, g, w, col0, ncols, cos, sin, *, eps, scale, bm, bn):
    t, d = x.shape
    cb = col0 // bn
    return pl.pallas_call(
        functools.partial(_norm_rope_kernel, eps=eps, scale=scale, heads_per_blk=bn // LANES),
        out_shape=(jax.ShapeDtypeStruct((t, ncols), F32), jax.ShapeDtypeStruct((t, ncols), BF16)),
        grid=(t // bm, ncols // bn),
        in_specs=[
            pl.BlockSpec((bm, d), lambda i, j: (i, 0)),
            pl.BlockSpec((1, d), lambda i, j: (0, 0)),
            pl.BlockSpec((d, bn), lambda i, j: (0, cb + j)),
            pl.BlockSpec((bm, LANES), lambda i, j: (i, 0)),
            pl.BlockSpec((bm, LANES), lambda i, j: (i, 0)),
        ],
        out_specs=(pl.BlockSpec((bm, bn), lambda i, j: (i, j)), pl.BlockSpec((bm, bn), lambda i, j: (i, j))),
        scratch_shapes=[pltpu.VMEM((bm, d), BF16)],
        compiler_params=_params(),
    )(x, g.reshape(1, d), w, cos, sin)


def _matmul_residual_kernel(a_ref, w_ref, r_ref, o_ref):
    o_ref[...] = r_ref[...] + _dot(a_ref[...], w_ref[...].astype(BF16))


def _matmul_residual(a, w, res, *, bm, bn):
    t, k = a.shape
    n = w.shape[1]
    return pl.pallas_call(
        _matmul_residual_kernel,
        out_shape=jax.ShapeDtypeStruct((t, n), F32),
        grid=(t // bm, n // bn),
        in_specs=[
            pl.BlockSpec((bm, k), lambda i, j: (i, 0)),
            pl.BlockSpec((k, bn), lambda i, j: (0, j)),
            pl.BlockSpec((bm, bn), lambda i, j: (i, j)),
        ],
        out_specs=pl.BlockSpec((bm, bn), lambda i, j: (i, j)),
        compiler_params=_params(),
    )(a, w, res)


def _ffn_kernel(x_ref, g_ref, wg_ref, wu_ref, wd_ref, o_ref, h_ref, *, eps):
    @pl.when(pl.program_id(1) == 0)
    def _():
        x = x_ref[...]
        h_ref[...] = _rms(x, g_ref[...], eps).astype(BF16)
        o_ref[...] = x

    h = h_ref[...]
    gate = _dot(h, wg_ref[...].astype(BF16))
    up = _dot(h, wu_ref[...].astype(BF16))
    act = (_silu(gate) * up).astype(BF16)
    o_ref[...] += _dot(act, wd_ref[...].astype(BF16))


def _ffn_dense(y, g, w_gu, w_down, *, eps, bm, bf):
    t, d = y.shape
    dff = w_down.shape[0]
    nf = dff // bf
    return pl.pallas_call(
        functools.partial(_ffn_kernel, eps=eps),
        out_shape=jax.ShapeDtypeStruct((t, d), F32),
        grid=(t // bm, nf),
        in_specs=[
            pl.BlockSpec((bm, d), lambda i, f: (i, 0)),
            pl.BlockSpec((1, d), lambda i, f: (0, 0)),
            pl.BlockSpec((d, bf), lambda i, f: (0, f)),
            pl.BlockSpec((d, bf), lambda i, f: (0, nf + f)),
            pl.BlockSpec((bf, d), lambda i, f: (f, 0)),
        ],
        out_specs=pl.BlockSpec((bm, d), lambda i, f: (i, 0)),
        scratch_shapes=[pltpu.VMEM((bm, d), BF16)],
        compiler_params=_params(),
    )(y, g.reshape(1, d), w_gu, w_gu, w_down)


def _hgrn_consts(c, seg):
    t = np.arange(c)[:, None]
    p = np.arange(c)[None, :]
    same = (t // seg) == (p // seg)
    levels = []
    m = seg // 2
    while m >= 1:
        levels.append(m)
        m //= 2
    blocks = [(p <= t) & same, (p > t) & same]
    masks = []
    for m in levels:
        r = (t // (2 * m)) * (2 * m) + m - 1
        upper = (t % (2 * m)) >= m
        blocks.append(np.where(upper, (p > r) & (p <= t), (p > t) & (p <= r)))
        masks.append(((t // (2 * m)) == (p // (2 * m))) & upper & ((p % (2 * m)) < m))
    masks.append(t == p)
    w = np.concatenate(blocks, axis=0).astype(np.float32)
    return jnp.asarray(w, BF16), jnp.asarray(np.stack(masks).astype(np.float32)), len(levels)


def _hgrn_lower_bound(logits, layer):
    e = jnp.exp(logits - jnp.max(logits, axis=0, keepdims=True))
    sm = e / jnp.sum(e, axis=0, keepdims=True)
    return jnp.sum(sm[: layer + 1], axis=0, keepdims=True)


def _hgrn_gates(qp, fp, lb, kdim):
    q = _silu(qp) * (kdim ** -0.5)
    f = lb + (1.0 - lb) * _sigmoid(fp)
    return q, jnp.log(f), 1.0 - f


def _hgrn_intra(q, k, lg, w, m_ref, c, nlev):
    hi, mid, lo = _split3(lg)
    x = jnp.exp(_dot(w, hi) + _dot(w, mid) + _dot(w, lo))
    a = m_ref[nlev] * _dot_nt(q.astype(BF16), k.astype(BF16))
    for li in range(nlev):
        xm = x[(2 + li) * c:(3 + li) * c]
        a = a + m_ref[li] * _dot_nt((q * xm).astype(BF16), (k * xm).astype(BF16))
    return x, a


def _hgrn_prompt_kernel(q_ref, f_ref, i_ref, g_ref, lb_ref, ng_ref, w_ref, m_ref, o_ref, s_ref, st_ref,
                        *, hb, c, nlev, layer, eps):
    ci = pl.program_id(2)

    @pl.when(ci == 0)
    def _():
        st_ref[...] = jnp.zeros_like(st_ref)

    w = w_ref[...]
    for hh in range(hb):
        sl = slice(hh * LANES, (hh + 1) * LANES)
        lb = _hgrn_lower_bound(lb_ref[:, sl], layer)
        q, lg, k = _hgrn_gates(q_ref[:, sl], f_ref[:, sl], lb, LANES)
        x, a = _hgrn_intra(q, k, lg, w, m_ref, c, nlev)
        vb = i_ref[:, sl].astype(BF16)
        st = st_ref[hh]
        o = _dot_nt((q * x[0:c]).astype(BF16), st.astype(BF16)) + _dot(a.astype(BF16), vb)
        kl = (k * x[c:2 * c]).astype(BF16)
        st_ref[hh] = st * x[c - 1:c] + _dot_tn(vb, kl)
        o_ref[:, sl] = (_rms(o, ng_ref[...], eps) * _silu(g_ref[:, sl])).astype(o_ref.dtype)

    @pl.when(ci == pl.num_programs(2) - 1)
    def _():
        for hh in range(hb):
            s_ref[0, hh] = st_ref[hh].T


def _hgrn_prompt(proj, lb_logits, norm_g, n_seq, seq_len, *, layer, c, hb):
    hk = proj.shape[1] // 4
    nh = hk // LANES
    nc = seq_len // c
    ngrp = nh // hb
    w, masks, nlev = _hgrn_consts(c, c)
    bw = hb * LANES

    def col(part):
        return pl.BlockSpec((c, bw), lambda b, h, ci: (b * nc + ci, part * ngrp + h))

    return pl.pallas_call(
        functools.partial(_hgrn_prompt_kernel, hb=hb, c=c, nlev=nlev, layer=layer, eps=NORM_EPS),
        out_shape=(jax.ShapeDtypeStruct((n_seq * seq_len, hk), BF16),
                   jax.ShapeDtypeStruct((n_seq, nh, LANES, LANES), F32)),
        grid=(n_seq, ngrp, nc),
        in_specs=[
            col(0), col(1), col(2), col(3),
            pl.BlockSpec((lb_logits.shape[0], bw), lambda b, h, ci: (0, h)),
            pl.BlockSpec((1, LANES), lambda b, h, ci: (0, 0)),
            pl.BlockSpec(w.shape, lambda b, h, ci: (0, 0)),
            pl.BlockSpec(masks.shape, lambda b, h, ci: (0, 0, 0)),
        ],
        out_specs=(pl.BlockSpec((c, bw), lambda b, h, ci: (b * nc + ci, h)),
                   pl.BlockSpec((1, hb, LANES, LANES), lambda b, h, ci: (b, h, 0, 0))),
        scratch_shapes=[pltpu.VMEM((hb, LANES, LANES), F32)],
        compiler_params=_params(),
    )(proj, proj, proj, proj, lb_logits, norm_g.reshape(1, LANES), w, masks)


def _hgrn_sample_kernel(q_ref, f_ref, i_ref, g_ref, lb_ref, ng_ref, w_ref, m_ref, s0_ref, o_ref, s_ref,
                        *, hb, nb, tq, nlev, layer, eps):
    c = nb * tq
    w = w_ref[...]
    seq_of_row = lax.broadcasted_iota(jnp.int32, (c, 1), 0) // tq
    for hh in range(hb):
        sl = slice(hh * LANES, (hh + 1) * LANES)
        lb = _hgrn_lower_bound(lb_ref[:, sl], layer)
        q, lg, k = _hgrn_gates(q_ref[:, sl], f_ref[:, sl], lb, LANES)
        x, a = _hgrn_intra(q, k, lg, w, m_ref, c, nlev)
        vb = i_ref[:, sl].astype(BF16)
        q0 = (q * x[0:c]).astype(BF16)
        kl = k * x[c:2 * c]
        o = _dot(a.astype(BF16), vb)
        for b in range(nb):
            mine = seq_of_row == b
            st = s0_ref[b, hh].T
            o = o + jnp.where(mine, _dot_nt(q0, st.astype(BF16)), 0.0)
            r_last = (b + 1) * tq - 1
            klb = jnp.where(mine, kl, 0.0).astype(BF16)
            s_ref[b, hh] = (st * x[r_last:r_last + 1] + _dot_tn(vb, klb)).T
        o_ref[:, sl] = (_rms(o, ng_ref[...], eps) * _silu(g_ref[:, sl])).astype(o_ref.dtype)


def _hgrn_sample(proj, row0, state, lb_logits, norm_g, tq, *, layer, nb, hb):
    n_seq, nh = state.shape[0], state.shape[1]
    hk = proj.shape[1] // 4
    ngrp = nh // hb
    c = nb * tq
    rb0 = row0 // c
    w, masks, nlev = _hgrn_consts(c, tq)
    bw = hb * LANES

    def col(part):
        return pl.BlockSpec((c, bw), lambda sb, h: (rb0 + sb, part * ngrp + h))

    return pl.pallas_call(
        functools.partial(_hgrn_sample_kernel, hb=hb, nb=nb, tq=tq, nlev=nlev, layer=layer, eps=NORM_EPS),
        out_shape=(jax.ShapeDtypeStruct((n_seq * tq, hk), BF16),
                   jax.ShapeDtypeStruct(state.shape, F32)),
        grid=(n_seq // nb, ngrp),
        in_specs=[
            col(0), col(1), col(2), col(3),
            pl.BlockSpec((lb_logits.shape[0], bw), lambda sb, h: (0, h)),
            pl.BlockSpec((1, LANES), lambda sb, h: (0, 0)),
            pl.BlockSpec(w.shape, lambda sb, h: (0, 0)),
            pl.BlockSpec(masks.shape, lambda sb, h: (0, 0, 0)),
            pl.BlockSpec((nb, hb, LANES, LANES), lambda sb, h: (sb, h, 0, 0)),
        ],
        out_specs=(pl.BlockSpec((c, bw), lambda sb, h: (sb, h)),
                   pl.BlockSpec((nb, hb, LANES, LANES), lambda sb, h: (sb, h, 0, 0))),
        compiler_params=_params(),
    )(proj, proj, proj, proj, lb_logits, norm_g.reshape(1, LANES), w, masks, state)


def _diff_lambda(lam_ref, lam_init):
    lp = lam_ref[...]
    s1 = jnp.sum(lp[0:1] * lp[1:2], axis=-1, keepdims=True)
    s2 = jnp.sum(lp[2:3] * lp[3:4], axis=-1, keepdims=True)
    return jnp.exp(s1) - jnp.exp(s2) + lam_init


def _lanes(x, width):
    return x if width == LANES else jnp.concatenate([x] * (width // LANES), axis=1)


def _attn_prompt_kernel(qi_ref, kj_ref, lam_ref, sg_ref, q_ref, k_ref, v_ref, o_ref, m_ref, l_ref, acc_ref,
                        *, blk, lam_init, eps):
    pair = pl.program_id(2)
    i = qi_ref[pair]
    j = kj_ref[pair]
    hv = 2 * LANES

    @pl.when(j == 0)
    def _():
        m_ref[...] = jnp.full_like(m_ref, NEG_BIG)
        l_ref[...] = jnp.zeros_like(l_ref)
        acc_ref[...] = jnp.zeros_like(acc_ref)

    def step(diagonal):
        vb = v_ref[...]
        for s in range(2):
            sl = slice(s * LANES, (s + 1) * LANES)
            sc = _dot_nt(q_ref[:, sl], k_ref[:, sl])
            if diagonal:
                row = lax.broadcasted_iota(jnp.int32, (blk, blk), 0)
                colv = lax.broadcasted_iota(jnp.int32, (blk, blk), 1)
                sc = jnp.where(colv <= row, sc, NEG_BIG)
            m_prev = m_ref[s]
            m_new = jnp.maximum(m_prev, jnp.max(sc, axis=-1, keepdims=True))
            alpha = jnp.exp(m_prev - m_new)
            p = jnp.exp(sc - _lanes(m_new, blk))
            l_ref[s] = alpha * l_ref[s] + jnp.sum(p, axis=-1, keepdims=True)
            acc_ref[s] = _lanes(alpha, hv) * acc_ref[s] + _dot(p.astype(BF16), vb)
            m_ref[s] = m_new

    @pl.when(j < i)
    def _():
        step(False)

    @pl.when(j == i)
    def _():
        step(True)
        lam = _diff_lambda(lam_ref, lam_init)
        o = acc_ref[0] / _lanes(l_ref[0], hv) - lam * (acc_ref[1] / _lanes(l_ref[1], hv))
        o_ref[...] = (_rms(o, sg_ref[...], eps) * (1.0 - lam_init)).astype(o_ref.dtype)


def _attn_prompt(q, k, v, lam_p, subln_g, n_seq, seq_len, *, lam_init, blk):
    hv = 2 * LANES
    nh = q.shape[1] // hv
    nb = seq_len // blk
    pairs = [(i, j) for i in range(nb) for j in range(i + 1)]
    qi = jnp.asarray([p[0] for p in pairs], jnp.int32)
    kj = jnp.asarray([p[1] for p in pairs], jnp.int32)
    return pl.pallas_call(
        functools.partial(_attn_prompt_kernel, blk=blk, lam_init=lam_init, eps=SUBLN_EPS),
        out_shape=jax.ShapeDtypeStruct((n_seq * seq_len, nh * hv), BF16),
        grid_spec=pltpu.PrefetchScalarGridSpec(
            num_scalar_prefetch=2,
            grid=(n_seq, nh, len(pairs)),
            in_specs=[
                pl.BlockSpec(lam_p.shape, lambda b, h, p, qi, kj: (0, 0)),
                pl.BlockSpec((1, hv), lambda b, h, p, qi, kj: (0, 0)),
                pl.BlockSpec((blk, hv), lambda b, h, p, qi, kj: (b * nb + qi[p], h)),
                pl.BlockSpec((blk, hv), lambda b, h, p, qi, kj: (b * nb + kj[p], h)),
                pl.BlockSpec((blk, hv), lambda b, h, p, qi, kj: (b * nb + kj[p], h)),
            ],
            out_specs=pl.BlockSpec((blk, hv), lambda b, h, p, qi, kj: (b * nb + qi[p], h)),
            scratch_shapes=[pltpu.VMEM((2, blk, LANES), F32), pltpu.VMEM((2, blk, LANES), F32),
                            pltpu.VMEM((2, blk, hv), F32)],
        ),
        compiler_params=_params(),
    )(qi, kj, lam_p, subln_g.reshape(1, hv), q, k, v)


def _attn_sample_kernel(pt_ref, lam_ref, sg_ref, q_ref, kn_ref, vn_ref, *rest, pages, tq, nh, lam_init, eps):
    k_refs = rest[:pages]
    v_refs = rest[pages:3 * pages]
    o_ref, m_ref, l_ref, acc_ref = rest[3 * pages:]
    g = pl.program_id(1)
    hv = 2 * LANES
    rows = 2 * tq

    @pl.when(g == 0)
    def _():
        m_ref[...] = jnp.full_like(m_ref, NEG_BIG)
        l_ref[...] = jnp.zeros_like(l_ref)
        acc_ref[...] = jnp.zeros_like(acc_ref)

    first = lax.broadcasted_iota(jnp.int32, (rows, 1), 0) < tq

    def scores(k_sub):
        out = []
        for h in range(nh):
            qa = q_ref[h * rows:(h + 1) * rows, :]
            out.append(jnp.where(first, _dot_nt(qa, k_sub(2 * h)), _dot_nt(qa, k_sub(2 * h + 1))))
        return jnp.concatenate(out, axis=0)

    def update(sc, v_head):
        m_prev = m_ref[...]
        m_new = jnp.maximum(m_prev, jnp.max(sc, axis=-1, keepdims=True))
        alpha = jnp.exp(m_prev - m_new)
        p = jnp.exp(sc - m_new)
        l_ref[...] = alpha * l_ref[...] + jnp.sum(p, axis=-1, keepdims=True)
        pb = p.astype(BF16)
        pv = [_dot(pb[h * rows:(h + 1) * rows], v_head(h)) for h in range(nh)]
        acc_ref[...] = alpha * acc_ref[...] + jnp.concatenate(pv, axis=0)
        m_ref[...] = m_new

    n_sub = 2 * nh
    for pg in range(pages):
        k_pg, v_lo, v_hi = k_refs[pg], v_refs[2 * pg], v_refs[2 * pg + 1]
        psz = k_pg.shape[0] // n_sub

        def k_sub(j, r=k_pg):
            return r[pl.ds(j, psz, stride=n_sub), :].astype(BF16)

        def v_head(h, lo=v_lo, hi=v_hi):
            return jnp.concatenate([lo[pl.ds(h, psz, stride=nh), :], hi[pl.ds(h, psz, stride=nh), :]],
                                   axis=1).astype(BF16)

        update(scores(k_sub), v_head)

    @pl.when(g == pl.num_programs(1) - 1)
    def _():
        pad = jnp.zeros((LANES - tq, kn_ref.shape[1]), F32)
        kb = jnp.concatenate([kn_ref[...], pad], axis=0).astype(BF16)
        vb = jnp.concatenate([vn_ref[...], pad], axis=0).astype(BF16)
        sc = scores(lambda j: kb[:, j * LANES:(j + 1) * LANES])
        qpos = lax.broadcasted_iota(jnp.int32, sc.shape, 0) % tq
        kpos = lax.broadcasted_iota(jnp.int32, sc.shape, 1)
        update(jnp.where(kpos <= qpos, sc, NEG_BIG), lambda h: vb[:, h * hv:(h + 1) * hv])
        lam = _diff_lambda(lam_ref, lam_init)
        attn = acc_ref[...] / l_ref[...]
        sg = sg_ref[...]
        for h in range(nh):
            o = attn[h * rows:h * rows + tq] - lam * attn[h * rows + tq:(h + 1) * rows]
            o_ref[:, h * hv:(h + 1) * hv] = _rms(o, sg, eps) * (1.0 - lam_init)


def _attn_sample(q16, k_new, v_new, row0, cache_k, cache_v, page_table, lam_p, subln_g, tq, *, lam_init, pages):
    n_seq, n_pages = page_table.shape
    hv = 2 * LANES
    width = k_new.shape[1]
    nh = width // hv
    rows = nh * 2 * tq
    rb0 = row0 // tq

    def page_spec(cache, pg, lane_blk):
        return pl.BlockSpec((None, cache.shape[1], LANES),
                            lambda b, g, pt: (pt[b * n_pages + g * pages + pg], 0, lane_blk))

    return pl.pallas_call(
        functools.partial(_attn_sample_kernel, pages=pages, tq=tq, nh=nh, lam_init=lam_init, eps=SUBLN_EPS),
        out_shape=jax.ShapeDtypeStruct((n_seq, tq, width), F32),
        grid_spec=pltpu.PrefetchScalarGridSpec(
            num_scalar_prefetch=1,
            grid=(n_seq, n_pages // pages),
            in_specs=[
                pl.BlockSpec(lam_p.shape, lambda b, g, pt: (0, 0)),
                pl.BlockSpec((1, hv), lambda b, g, pt: (0, 0)),
                pl.BlockSpec((None, rows, LANES), lambda b, g, pt: (b, 0, 0)),
                pl.BlockSpec((tq, width), lambda b, g, pt: (rb0 + b, 0)),
                pl.BlockSpec((tq, width), lambda b, g, pt: (rb0 + b, 0)),
            ] + [page_spec(cache_k, pg, 0) for pg in range(pages)]
              + [page_spec(cache_v, pg, half) for pg in range(pages) for half in range(2)],
            out_specs=pl.BlockSpec((None, tq, width), lambda b, g, pt: (b, 0, 0)),
            scratch_shapes=[pltpu.VMEM((rows, 1), F32), pltpu.VMEM((rows, 1), F32), pltpu.VMEM((rows, hv), F32)],
        ),
        compiler_params=_params(),
    )(page_table.reshape(-1), lam_p, subln_g.reshape(1, hv), q16, k_new, v_new,
      *([cache_k] * pages), *([cache_v] * (2 * pages)))


def _moe_route_kernel(y_ref, g_ref, wr_ref, hn_ref, r_ref, *, eps, n_exp):
    h = _rms(y_ref[...], g_ref[...], eps)
    hn_ref[...] = h
    h1, h2, h3 = _split3(h)
    w1, w2, w3 = _split3(wr_ref[...])
    logits = (_dot(h1, w1) + (_dot(h1, w2) + _dot(h2, w1)) + (_dot(h1, w3) + _dot(h2, w2) + _dot(h3, w1)))
    lane = lax.broadcasted_iota(jnp.int32, logits.shape, 1)
    logits = jnp.where(lane < n_exp, logits, NEG_BIG)
    m1 = jnp.max(logits, axis=-1, keepdims=True)
    i1 = jnp.min(jnp.where(logits == m1, lane, LANES), axis=-1, keepdims=True)
    rest = jnp.where(lane == i1, NEG_BIG, logits)
    m2 = jnp.max(rest, axis=-1, keepdims=True)
    i2 = jnp.min(jnp.where(rest == m2, lane, LANES), axis=-1, keepdims=True)
    e2 = jnp.exp(m2 - m1)
    w_top1 = 1.0 / (1.0 + e2)
    w_top2 = e2 / (1.0 + e2)
    r_ref[...] = jnp.where(lane == 0, i1.astype(F32),
                           jnp.where(lane == 1, i2.astype(F32),
                                     jnp.where(lane == 2, w_top1, jnp.where(lane == 3, w_top2, 0.0))))


def _moe_route(y, g, w_router, *, bm):
    t, d = y.shape
    n_exp = w_router.shape[1]
    wr = jnp.pad(w_router, ((0, 0), (0, LANES - n_exp)))
    return pl.pallas_call(
        functools.partial(_moe_route_kernel, eps=NORM_EPS, n_exp=n_exp),
        out_shape=(jax.ShapeDtypeStruct((t, d), F32), jax.ShapeDtypeStruct((t, LANES), F32)),
        grid=(t // bm,),
        in_specs=[
            pl.BlockSpec((bm, d), lambda i: (i, 0)),
            pl.BlockSpec((1, d), lambda i: (0, 0)),
            pl.BlockSpec((d, LANES), lambda i: (0, 0)),
        ],
        out_specs=(pl.BlockSpec((bm, d), lambda i: (i, 0)), pl.BlockSpec((bm, LANES), lambda i: (i, 0))),
        compiler_params=_params(),
    )(y, g.reshape(1, d), wr)


def _row_copy(src_hbm, dst, src_row, dst_row, sem):
    return pltpu.make_async_copy(src_hbm.at[pl.ds(src_row, 1)], dst.at[pl.ds(dst_row, 1)], sem)


def _moe_expert_kernel(te_ref, nr_ref, tok_ref, hn_ref, gate_ref, wg_ref, wu_ref, wd_ref, o_ref, xin_ref, xb_ref, sem,
                       *, bm, sub):
    i = pl.program_id(0)
    f = pl.program_id(1)
    nrows = nr_ref[i]

    def issue(tile):
        def body(r, carry):
            _row_copy(hn_ref, xin_ref, tok_ref[tile * bm + r], r, sem).start()
            return carry
        lax.fori_loop(0, nr_ref[tile], body, 0)

    @pl.when(f == 0)
    def _():
        @pl.when(i == 0)
        def _():
            xin_ref[...] = jnp.zeros_like(xin_ref)
            issue(0)

        def drain(r, carry):
            _row_copy(hn_ref, xin_ref, 0, r, sem).wait()
            return carry
        lax.fori_loop(0, nrows, drain, 0)
        row = lax.broadcasted_iota(jnp.int32, (bm, 1), 0)
        xb_ref[...] = jnp.where(row < nrows, xin_ref[...], 0.0).astype(BF16)
        o_ref[...] = jnp.zeros_like(o_ref)

        @pl.when(i + 1 < pl.num_programs(0))
        def _():
            issue(i + 1)

    for m_rows in range(sub, bm + 1, sub):
        @pl.when((nrows > m_rows - sub) & (nrows <= m_rows))
        def _(m_rows=m_rows):
            x = xb_ref[0:m_rows]
            gate = _dot(x, wg_ref[...].astype(BF16))
            up = _dot(x, wu_ref[...].astype(BF16))
            act = (_silu(gate) * up).astype(BF16)
            o_ref[0:m_rows] += _dot(act, wd_ref[...].astype(BF16))

    @pl.when(f == pl.num_programs(1) - 1)
    def _():
        o_ref[...] = o_ref[...] * gate_ref[...]


def _moe_experts(hn, token_sorted, gate_sorted, tile_expert, tile_rows, w_gu, w_down, *, bm, bf, sub):
    d = hn.shape[1]
    n_tiles = tile_expert.shape[0]
    dff = w_down.shape[1]
    nf = dff // bf

    def fblk(i, f, nr):
        return jnp.where(nr[i] > 0, f, nf - 1)

    return pl.pallas_call(
        functools.partial(_moe_expert_kernel, bm=bm, sub=sub),
        out_shape=jax.ShapeDtypeStruct((n_tiles * bm, d), F32),
        grid_spec=pltpu.PrefetchScalarGridSpec(
            num_scalar_prefetch=3,
            grid=(n_tiles, nf),
            in_specs=[
                pl.BlockSpec(memory_space=pl.ANY),
                pl.BlockSpec((bm, 1), lambda i, f, te, nr, tok: (i, 0)),
                pl.BlockSpec((None, d, bf), lambda i, f, te, nr, tok: (te[i], 0, fblk(i, f, nr))),
                pl.BlockSpec((None, d, bf), lambda i, f, te, nr, tok: (te[i], 0, nf + fblk(i, f, nr))),
                pl.BlockSpec((None, bf, d), lambda i, f, te, nr, tok: (te[i], fblk(i, f, nr), 0)),
            ],
            out_specs=pl.BlockSpec((bm, d), lambda i, f, te, nr, tok: (i, 0)),
            scratch_shapes=[pltpu.VMEM((bm, d), F32), pltpu.VMEM((bm, d), BF16), pltpu.SemaphoreType.DMA],
        ),
        compiler_params=_params(),
    )(tile_expert, tile_rows, token_sorted, hn, gate_sorted.reshape(n_tiles * bm, 1), w_gu, w_gu, w_down)


def _moe_combine_kernel(pos_ref, y_ref, g_ref, ys_ref, o_ref, buf_ref, sem, *, bt, eps):
    def issue(r, carry):
        _row_copy(ys_ref, buf_ref.at[0], pos_ref[0, 0, 2 * r], r, sem).start()
        _row_copy(ys_ref, buf_ref.at[1], pos_ref[0, 0, 2 * r + 1], r, sem).start()
        return carry

    lax.fori_loop(0, bt, issue, 0)

    def drain(r, carry):
        _row_copy(ys_ref, buf_ref.at[0], 0, r, sem).wait()
        _row_copy(ys_ref, buf_ref.at[1], 0, r, sem).wait()
        return carry

    lax.fori_loop(0, bt, drain, 0)
    z = y_ref[...] + (buf_ref[0] + buf_ref[1])
    o_ref[...] = _rms(z, g_ref[...], eps)


def _moe_combine_norm(y, ys, pos, g, *, bt):
    t, d = y.shape
    steps = t // bt
    return pl.pallas_call(
        functools.partial(_moe_combine_kernel, bt=bt, eps=NORM_EPS),
        out_shape=jax.ShapeDtypeStruct((t, d), F32),
        grid=(steps,),
        in_specs=[
            pl.BlockSpec((1, 1, 2 * bt), lambda i: (i, 0, 0), memory_space=pltpu.SMEM),
            pl.BlockSpec((bt, d), lambda i: (i, 0)),
            pl.BlockSpec((1, d), lambda i: (0, 0)),
            pl.BlockSpec(memory_space=pl.ANY),
        ],
        out_specs=pl.BlockSpec((bt, d), lambda i: (i, 0)),
        scratch_shapes=[pltpu.VMEM((2, bt, d), F32), pltpu.SemaphoreType.DMA],
        compiler_params=_params(),
    )(pos.reshape(steps, 1, 2 * bt), y, g.reshape(1, d), ys)


def _moe_plan(route, n_exp, bm, n_tiles):
    t = route.shape[0]
    experts = route[:, 0:2].astype(jnp.int32).reshape(-1)
    gates = route[:, 2:4].reshape(-1)
    onehot = (experts[:, None] == jnp.arange(n_exp, dtype=jnp.int32)[None, :]).astype(jnp.int32)
    rank = jnp.sum((jnp.cumsum(onehot, axis=0) - onehot) * onehot, axis=1)
    counts = jnp.sum(onehot, axis=0)
    tiles_per = (counts + bm - 1) // bm
    rows_per = (((counts + jnp.maximum(tiles_per, 1) - 1) // jnp.maximum(tiles_per, 1)) + 7) // 8 * 8
    rows_per = jnp.maximum(rows_per, 8)
    tile_end = jnp.cumsum(tiles_per)
    tile_start = tile_end - tiles_per
    dest = (tile_start[experts] + rank // rows_per[experts]) * bm + rank % rows_per[experts]
    p = n_tiles * bm
    token_sorted = jnp.zeros((p,), jnp.int32).at[dest].set(jnp.arange(2 * t, dtype=jnp.int32) // 2)
    gate_sorted = jnp.zeros((p,), F32).at[dest].set(gates)
    n_used = tile_end[-1]
    tiles = jnp.arange(n_tiles, dtype=jnp.int32)
    tile_expert = jnp.sum((jnp.minimum(tiles, n_used - 1)[:, None] >= tile_end[None, :]).astype(jnp.int32), axis=1)
    local = tiles - tile_start[tile_expert]
    tile_rows = jnp.clip(counts[tile_expert] - local * rows_per[tile_expert], 0, rows_per[tile_expert])
    tile_rows = jnp.where(tiles < n_used, tile_rows, 0)
    return token_sorted, gate_sorted, tile_expert.astype(jnp.int32), tile_rows.astype(jnp.int32), dest


def _moe_block_norm(y, g_ffn, w_router, w_gu, w_down, g_out, *, bm_route, bm, bf, sub, bt):
    t = y.shape[0]
    n_exp = w_router.shape[1]
    hn, route = _moe_route(y, g_ffn, w_router, bm=bm_route)
    n_tiles = (2 * t) // bm + n_exp
    token_sorted, gate_sorted, tile_expert, tile_rows, dest = _moe_plan(route, n_exp, bm, n_tiles)
    ys = _moe_experts(hn, token_sorted, gate_sorted, tile_expert, tile_rows, w_gu, w_down, bm=bm, bf=bf, sub=sub)
    return _moe_combine_norm(y, ys, dest, g_out, bt=bt)


def _rope_tables(positions, half):
    inv = ROPE_THETA ** (-jnp.arange(half, dtype=F32) / half)
    ang = positions.astype(F32)[:, None] * inv[None, :]
    cos, sin = jnp.cos(ang), jnp.sin(ang)
    return jnp.concatenate([cos, cos], axis=1), jnp.concatenate([-sin, sin], axis=1)


def _step(x_prompt, x_sample, cache_k, cache_v, state_hgrn, page_table, norm_mix_g, norm_ffn_g, norm_out_g,
          hg_w_in, hg_lb_logits, hg_norm_g, hg_w_out, da_w_in, da_lambda, da_subln_g, da_w_out,
          ffn_w_gu, ffn_w_down, moe_w_router, moe_w_gu, moe_w_down, *, cfg):
    nb_p, seq, d = x_prompt.shape
    nb_s, tq, _ = x_sample.shape
    tp, ts = nb_p * seq, nb_s * tq
    past = page_table.shape[1] * cache_k.shape[2]
    dh = cache_k.shape[-1]
    n_sub = cache_k.shape[-2]
    nh_attn = n_sub // 2
    y = jnp.concatenate([x_prompt.reshape(tp, d), x_sample.reshape(ts, d)], axis=0)

    hk = hg_w_in.shape[2] // 4
    (proj,) = _norm_matmul(y, norm_mix_g[0], hg_w_in[0], 0, 4 * hk, eps=NORM_EPS, bm=cfg["bm"], bn=cfg["bn"])
    o_p, st_p = _hgrn_prompt(proj, hg_lb_logits, hg_norm_g[0], nb_p, seq, layer=0, c=cfg["hg_chunk"], hb=cfg["hg_hb"])
    o_s, st_s = _hgrn_sample(proj, tp, state_hgrn[0], hg_lb_logits, hg_norm_g[0], tq, layer=0,
                             nb=cfg["hg_nb"], hb=cfg["hg_hb"])
    y = _matmul_residual(jnp.concatenate([o_p, o_s], axis=0), hg_w_out[0], y, bm=cfg["bm"], bn=cfg["bn"])
    y = _ffn_dense(y, norm_ffn_g[0], ffn_w_gu[0], ffn_w_down[0], eps=NORM_EPS, bm=cfg["ffn_bm"], bf=cfg["ffn_bf"])

    lam_init = 0.8 - 0.6 * math.exp(-0.3 * 1)
    pos = jnp.concatenate([jnp.tile(jnp.arange(seq, dtype=jnp.int32), nb_p),
                           jnp.tile(past + jnp.arange(tq, dtype=jnp.int32), nb_s)])
    cos, sin = _rope_tables(pos, dh // 2)
    qk = n_sub * dh
    g1 = norm_mix_g[1]
    w_in = da_w_in[0]
    _, q16 = _norm_rope(y, g1, w_in, 0, qk, cos, sin, eps=NORM_EPS, scale=dh ** -0.5, bm=cfg["bm"], bn=cfg["bn"])
    k32, k16 = _norm_rope(y, g1, w_in, qk, qk, cos, sin, eps=NORM_EPS, scale=1.0, bm=cfg["bm"], bn=cfg["bn"])
    v32, v16 = _norm_matmul(y, g1, w_in, 2 * qk, qk, eps=NORM_EPS, bm=cfg["bm"], bn=cfg["bn"], out_dtypes=(F32, BF16))
    a_p = _attn_prompt(q16, k16, v16, da_lambda[0], da_subln_g[0], nb_p, seq, lam_init=lam_init, blk=cfg["attn_blk"])
    q_s = q16[tp:].reshape(nb_s, tq, nh_attn, 2, dh).transpose(0, 2, 3, 1, 4).reshape(nb_s, nh_attn * 2 * tq, dh)
    ck = cache_k[0].reshape(cache_k.shape[1], cache_k.shape[2] * n_sub, dh)
    cv = cache_v[0].reshape(cache_v.shape[1], cache_v.shape[2] * nh_attn, 2 * dh)
    a_s = _attn_sample(q_s, k32, v32, tp, ck, cv, page_table, da_lambda[0], da_subln_g[0], tq,
                       lam_init=lam_init, pages=cfg["attn_pages"])
    attn = jnp.concatenate([a_p, a_s.reshape(ts, qk).astype(BF16)], axis=0)
    y = _matmul_residual(attn, da_w_out[0], y, bm=cfg["bm"], bn=cfg["bn"])
    y = _moe_block_norm(y, norm_ffn_g[1], moe_w_router[0], moe_w_gu[0], moe_w_down[0], norm_out_g,
                        bm_route=cfg["route_bm"], bm=cfg["moe_bm"], bf=cfg["moe_bf"],
                        sub=cfg["moe_sub"], bt=cfg["combine_bt"])

    return (y[:tp].reshape(nb_p, seq, d), y[tp:].reshape(nb_s, tq, d),
            k32[:tp].reshape(1, nb_p, seq, n_sub, dh), v32[:tp].reshape(1, nb_p, seq, nh_attn, 2 * dh),
            k32[tp:].reshape(1, nb_s, tq, n_sub, dh), v32[tp:].reshape(1, nb_s, tq, nh_attn, 2 * dh),
            st_p[None], st_s[None])


_CFG = dict(bm=1056, bn=512, hg_chunk=128, hg_hb=4, hg_nb=8, ffn_bm=768, ffn_bf=256, attn_blk=512,
            attn_pages=4, route_bm=264, moe_bm=768, moe_bf=512, moe_sub=256, combine_bt=264)


def kernel(x_prompt, x_sample, cache_k, cache_v, state_hgrn, page_table, norm_mix_g, norm_ffn_g, norm_out_g,
           hg_w_in, hg_lb_logits, hg_norm_g, hg_w_out, da_w_in, da_lambda, da_subln_g, da_w_out,
           ffn_w_gu, ffn_w_down, moe_w_router, moe_w_gu, moe_w_down):
    return _step(x_prompt, x_sample, cache_k, cache_v, state_hgrn, page_table, norm_mix_g, norm_ffn_g, norm_out_g,
                 hg_w_in, hg_lb_logits, hg_norm_g, hg_w_out, da_w_in, da_lambda, da_subln_g, da_w_out,
                 ffn_w_gu, ffn_w_down, moe_w_router, moe_w_gu, moe_w_down, cfg=_CFG)
```

```python
import functools
import math

import numpy as np
import jax
import jax.numpy as jnp
from jax import lax
from jax.experimental import pallas as pl
from jax.experimental.pallas import tpu as pltpu

F32 = jnp.float32
BF16 = jnp.bfloat16

LANES = 128
SUBLANES = 8
NORM_EPS = 1e-6
SUBLN_EPS = 1e-5
ROPE_THETA = 10000.0
NEG_BIG = -1e30
VMEM_LIMIT = 56 * 1024 * 1024


def _params(**kw):
    return pltpu.CompilerParams(vmem_limit_bytes=VMEM_LIMIT, **kw)


def _dot(a, b):
    return jnp.dot(a, b, preferred_element_type=F32)


def _dot_nt(a, b):
    return lax.dot_general(a, b, (((1,), (1,)), ((), ())), preferred_element_type=F32)


def _dot_tn(a, b):
    return lax.dot_general(a, b, (((0,), (0,)), ((), ())), preferred_element_type=F32)


def _sigmoid(x):
    return 1.0 / (1.0 + jnp.exp(-x))


def _silu(x):
    return x * _sigmoid(x)


def _rms(x, g, eps):
    return x * lax.rsqrt(jnp.mean(x * x, axis=-1, keepdims=True) + eps) * g


def _split2(x):
    hi = x.astype(BF16)
    return hi, (x - hi.astype(F32)).astype(BF16)


def _split3(x):
    hi = x.astype(BF16)
    r1 = x - hi.astype(F32)
    mid = r1.astype(BF16)
    lo = (r1 - mid.astype(F32)).astype(BF16)
    return hi, mid, lo


def _norm_proj_kernel(x_ref, g_ref, w_ref, *rest, eps, rope_scale, n_out):
    o_refs, h_ref = rest[-1 - n_out:-1], rest[-1]

    @pl.when(pl.program_id(1) == 0)
    def _():
        h_ref[...] = _rms(x_ref[...], g_ref[...], eps).astype(BF16)

    acc = _dot(h_ref[...], w_ref[...].astype(BF16))
    if rope_scale is None:
        for o_ref in o_refs:
            o_ref[...] = acc.astype(o_ref.dtype)
    else:
        cos = rest[0][...]
        sin = rest[1][...]
        for c in range(acc.shape[1] // LANES):
            sl = slice(c * LANES, (c + 1) * LANES)
            xh = acc[:, sl]
            r = (xh * cos + pltpu.roll(xh, LANES // 2, 1) * sin) * rope_scale
            for o_ref in o_refs:
                o_ref[:, sl] = r.astype(o_ref.dtype)


def _norm_proj(x, g, w, col0, ncols, row0, nrows, *, eps, bm, bn, out_dtypes, rope=None):
    d = x.shape[1]
    cb, rb = col0 // bn, row0 // bm
    in_specs = [
        pl.BlockSpec((bm, d), lambda i, j: (rb + i, 0)),
        pl.BlockSpec((1, d), lambda i, j: (0, 0)),
        pl.BlockSpec((d, bn), lambda i, j: (0, cb + j)),
    ]
    args = [x, g.reshape(1, d), w]
    scale = None
    if rope is not None:
        cos, sin, scale = rope
        period = cos.shape[0] // bm
        in_specs += [pl.BlockSpec((bm, LANES), lambda i, j: (i % period, 0))] * 2
        args += [cos, sin]
    return pl.pallas_call(
        functools.partial(_norm_proj_kernel, eps=eps, rope_scale=scale, n_out=len(out_dtypes)),
        out_shape=tuple(jax.ShapeDtypeStruct((nrows, ncols), dt) for dt in out_dtypes),
        grid=(nrows // bm, ncols // bn),
        in_specs=in_specs,
        out_specs=tuple(pl.BlockSpec((bm, bn), lambda i, j: (i, j)) for _ in out_dtypes),
        scratch_shapes=[pltpu.VMEM((bm, d), BF16)],
        compiler_params=_params(),
    )(*args)


def _matmul_residual_kernel(a_ref, w_ref, r_ref, o_ref):
    o_ref[...] = r_ref[...] + _dot(a_ref[...], w_ref[...].astype(BF16))


def _matmul_residual(a, w, res, *, bm, bn):
    t, k = a.shape
    n = w.shape[1]
    return pl.pallas_call(
        _matmul_residual_kernel,
        out_shape=jax.ShapeDtypeStruct((t, n), F32),
        grid=(t // bm, n // bn),
        in_specs=[
            pl.BlockSpec((bm, k), lambda i, j: (i, 0)),
            pl.BlockSpec((k, bn), lambda i, j: (0, j)),
            pl.BlockSpec((bm, bn), lambda i, j: (i, j)),
        ],
        out_specs=pl.BlockSpec((bm, bn), lambda i, j: (i, j)),
        compiler_params=_params(),
    )(a, w, res)


def _ffn_kernel(x_ref, g_ref, wg_ref, wu_ref, wd_ref, o_ref, h_ref, *, eps):
    @pl.when(pl.program_id(1) == 0)
    def _():
        x = x_ref[...]
        h_ref[...] = _rms(x, g_ref[...], eps).astype(BF16)
        o_ref[...] = x

    h = h_ref[...]
    gate = _dot(h, wg_ref[...].astype(BF16))
    up = _dot(h, wu_ref[...].astype(BF16))
    act = (_silu(gate) * up).astype(BF16)
    o_ref[...] += _dot(act, wd_ref[...].astype(BF16))


def _ffn_dense(y, g, w_gu, w_down, *, eps, bm, bf):
    t, d = y.shape
    dff = w_down.shape[0]
    nf = dff // bf
    return pl.pallas_call(
        functools.partial(_ffn_kernel, eps=eps),
        out_shape=jax.ShapeDtypeStruct((t, d), F32),
        grid=(t // bm, nf),
        in_specs=[
            pl.BlockSpec((bm, d), lambda i, f: (i, 0)),
            pl.BlockSpec((1, d), lambda i, f: (0, 0)),
            pl.BlockSpec((d, bf), lambda i, f: (0, f)),
            pl.BlockSpec((d, bf), lambda i, f: (0, nf + f)),
            pl.BlockSpec((bf, d), lambda i, f: (f, 0)),
        ],
        out_specs=pl.BlockSpec((bm, d), lambda i, f: (i, 0)),
        scratch_shapes=[pltpu.VMEM((bm, d), BF16)],
        compiler_params=_params(),
    )(y, g.reshape(1, d), w_gu, w_gu, w_down)


def _hgrn_levels(seg):
    levels = []
    m = seg // 2
    while m >= 1:
        levels.append(m)
        m //= 2
    return levels


def _hgrn_consts(c, seg):
    t = np.arange(c)[:, None]
    p = np.arange(c)[None, :]
    same = (t // seg) == (p // seg)
    blocks = [(p <= t) & same]
    masks = []
    for m in _hgrn_levels(seg):
        r = (t // (2 * m)) * (2 * m) + m - 1
        upper = (t % (2 * m)) >= m
        if 2 * m < SUBLANES:
            blocks.append(np.where(upper, (p > r) & (p <= t), (p > t) & (p <= r)))
        masks.append(((t // (2 * m)) == (p // (2 * m))) & upper & ((p % (2 * m)) < m))
    masks.append(t == p)
    w = np.concatenate(blocks, axis=0).astype(np.float32)
    return jnp.asarray(w, BF16), jnp.asarray(np.stack(masks).astype(np.float32))


def _hgrn_lower_bound(logits, layer):
    e = jnp.exp(logits - jnp.max(logits, axis=0, keepdims=True))
    sm = e / jnp.sum(e, axis=0, keepdims=True)
    return jnp.sum(sm[: layer + 1], axis=0, keepdims=True)


def _hgrn_gates(qp, fp, lb, kdim):
    q = _silu(qp) * (kdim ** -0.5)
    f = lb + (1.0 - lb) * _sigmoid(fp)
    return q, jnp.log(f), 1.0 - f


def _block_row(g, blk, off):
    parts = [jnp.broadcast_to(g[b * blk + off:b * blk + off + 1], (blk, g.shape[1])) for b in range(g.shape[0] // blk)]
    return parts[0] if len(parts) == 1 else jnp.concatenate(parts, axis=0)


def _hgrn_intra(q, k, lg, w, m_ref, seg):
    c = q.shape[0]
    levels = _hgrn_levels(seg)
    hi, lo = _split2(lg)
    e = _dot(w, hi) + _dot(w, lo)
    g = e[0:c]
    a = m_ref[len(levels)] * _dot_nt(q.astype(BF16), k.astype(BF16))
    short = 0
    for li, m in enumerate(levels):
        if 2 * m >= SUBLANES:
            xm = jnp.exp(-jnp.abs(g - _block_row(g, 2 * m, m - 1)))
        else:
            short += 1
            xm = jnp.exp(e[short * c:(short + 1) * c])
        a = a + m_ref[li] * _dot_nt((q * xm).astype(BF16), (k * xm).astype(BF16))
    return jnp.exp(g), jnp.exp(-jnp.abs(g - _block_row(g, seg, seg - 1))), a


def _hgrn_prompt_kernel(q_ref, f_ref, i_ref, g_ref, lb_ref, ng_ref, w_ref, m_ref, o_ref, s_ref, st_ref,
                        *, hb, c, layer, eps):
    ci = pl.program_id(2)

    @pl.when(ci == 0)
    def _():
        st_ref[...] = jnp.zeros_like(st_ref)

    w = w_ref[...]
    for hh in range(hb):
        sl = slice(hh * LANES, (hh + 1) * LANES)
        lb = _hgrn_lower_bound(lb_ref[:, sl], layer)
        q, lg, k = _hgrn_gates(q_ref[:, sl], f_ref[:, sl], lb, LANES)
        xq, xk, a = _hgrn_intra(q, k, lg, w, m_ref, c)
        vb = i_ref[:, sl].astype(BF16)
        st = st_ref[hh]
        o = _dot_nt((q * xq).astype(BF16), st.astype(BF16)) + _dot(a.astype(BF16), vb)
        st_ref[hh] = st * xq[c - 1:c] + _dot_tn(vb, (k * xk).astype(BF16))
        o_ref[:, sl] = (_rms(o, ng_ref[...], eps) * _silu(g_ref[:, sl])).astype(o_ref.dtype)

    @pl.when(ci == pl.num_programs(2) - 1)
    def _():
        for hh in range(hb):
            s_ref[0, hh] = st_ref[hh].T


def _hgrn_prompt(proj, lb_logits, norm_g, n_seq, seq_len, *, layer, c, hb):
    hk = proj.shape[1] // 4
    nh = hk // LANES
    nc = seq_len // c
    ngrp = nh // hb
    w, masks = _hgrn_consts(c, c)
    bw = hb * LANES

    def col(part):
        return pl.BlockSpec((c, bw), lambda b, h, ci: (b * nc + ci, part * ngrp + h))

    return pl.pallas_call(
        functools.partial(_hgrn_prompt_kernel, hb=hb, c=c, layer=layer, eps=NORM_EPS),
        out_shape=(jax.ShapeDtypeStruct((n_seq * seq_len, hk), BF16),
                   jax.ShapeDtypeStruct((n_seq, nh, LANES, LANES), F32)),
        grid=(n_seq, ngrp, nc),
        in_specs=[
            col(0), col(1), col(2), col(3),
            pl.BlockSpec((lb_logits.shape[0], bw), lambda b, h, ci: (0, h)),
            pl.BlockSpec((1, LANES), lambda b, h, ci: (0, 0)),
            pl.BlockSpec(w.shape, lambda b, h, ci: (0, 0)),
            pl.BlockSpec(masks.shape, lambda b, h, ci: (0, 0, 0)),
        ],
        out_specs=(pl.BlockSpec((c, bw), lambda b, h, ci: (b * nc + ci, h)),
                   pl.BlockSpec((1, hb, LANES, LANES), lambda b, h, ci: (b, h, 0, 0))),
        scratch_shapes=[pltpu.VMEM((hb, LANES, LANES), F32)],
        compiler_params=_params(),
    )(proj, proj, proj, proj, lb_logits, norm_g.reshape(1, LANES), w, masks)


def _hgrn_sample_kernel(q_ref, f_ref, i_ref, g_ref, lb_ref, ng_ref, w_ref, m_ref, s0_ref, o_ref, s_ref,
                        *, hb, nb, tq, layer, eps):
    c = nb * tq
    w = w_ref[...]
    seq_of_row = lax.broadcasted_iota(jnp.int32, (c, 1), 0) // tq
    for hh in range(hb):
        sl = slice(hh * LANES, (hh + 1) * LANES)
        lb = _hgrn_lower_bound(lb_ref[:, sl], layer)
        q, lg, k = _hgrn_gates(q_ref[:, sl], f_ref[:, sl], lb, LANES)
        xq, xk, a = _hgrn_intra(q, k, lg, w, m_ref, tq)
        vb = i_ref[:, sl].astype(BF16)
        q0 = (q * xq).astype(BF16)
        kl = k * xk
        o = _dot(a.astype(BF16), vb)
        for b in range(nb):
            mine = seq_of_row == b
            st = s0_ref[b, hh].T
            o = o + jnp.where(mine, _dot_nt(q0, st.astype(BF16)), 0.0)
            r_last = (b + 1) * tq - 1
            klb = jnp.where(mine, kl, 0.0).astype(BF16)
            s_ref[b, hh] = (st * xq[r_last:r_last + 1] + _dot_tn(vb, klb)).T
        o_ref[:, sl] = (_rms(o, ng_ref[...], eps) * _silu(g_ref[:, sl])).astype(o_ref.dtype)


def _hgrn_sample(proj, row0, state, lb_logits, norm_g, tq, *, layer, nb, hb):
    n_seq, nh = state.shape[0], state.shape[1]
    hk = proj.shape[1] // 4
    ngrp = nh // hb
    c = nb * tq
    rb0 = row0 // c
    w, masks = _hgrn_consts(c, tq)
    bw = hb * LANES

    def col(part):
        return pl.BlockSpec((c, bw), lambda sb, h: (rb0 + sb, part * ngrp + h))

    return pl.pallas_call(
        functools.partial(_hgrn_sample_kernel, hb=hb, nb=nb, tq=tq, layer=layer, eps=NORM_EPS),
        out_shape=(jax.ShapeDtypeStruct((n_seq * tq, hk), BF16),
                   jax.ShapeDtypeStruct(state.shape, F32)),
        grid=(n_seq // nb, ngrp),
        in_specs=[
            col(0), col(1), col(2), col(3),
            pl.BlockSpec((lb_logits.shape[0], bw), lambda sb, h: (0, h)),
            pl.BlockSpec((1, LANES), lambda sb, h: (0, 0)),
            pl.BlockSpec(w.shape, lambda sb, h: (0, 0)),
            pl.BlockSpec(masks.shape, lambda sb, h: (0, 0, 0)),
            pl.BlockSpec((nb, hb, LANES, LANES), lambda sb, h: (sb, h, 0, 0)),
        ],
        out_specs=(pl.BlockSpec((c, bw), lambda sb, h: (sb, h)),
                   pl.BlockSpec((nb, hb, LANES, LANES), lambda sb, h: (sb, h, 0, 0))),
        compiler_params=_params(),
    )(proj, proj, proj, proj, lb_logits, norm_g.reshape(1, LANES), w, masks, state)


def _diff_lambda(lam_ref, lam_init):
    lp = lam_ref[...]
    s1 = jnp.sum(lp[0:1] * lp[1:2], axis=-1, keepdims=True)
    s2 = jnp.sum(lp[2:3] * lp[3:4], axis=-1, keepdims=True)
    return jnp.exp(s1) - jnp.exp(s2) + lam_init


def _lanes(x, width):
    return x if width == LANES else jnp.concatenate([x] * (width // LANES), axis=1)


def _attn_prompt_kernel(qi_ref, kj_ref, lam_ref, sg_ref, q_ref, k_ref, v_ref, o_ref, m_ref, l_ref, acc_ref,
                        *, blk, lam_init, eps):
    pair = pl.program_id(2)
    i = qi_ref[pair]
    j = kj_ref[pair]
    hv = 2 * LANES

    @pl.when(j == 0)
    def _():
        m_ref[...] = jnp.full_like(m_ref, NEG_BIG)
        l_ref[...] = jnp.zeros_like(l_ref)
        acc_ref[...] = jnp.zeros_like(acc_ref)

    def step(diagonal):
        vb = v_ref[...]
        for s in range(2):
            sl = slice(s * LANES, (s + 1) * LANES)
            sc = _dot_nt(q_ref[:, sl], k_ref[:, sl])
            if diagonal:
                row = lax.broadcasted_iota(jnp.int32, (blk, blk), 0)
                colv = lax.broadcasted_iota(jnp.int32, (blk, blk), 1)
                sc = jnp.where(colv <= row, sc, NEG_BIG)
            m_prev = m_ref[s]
            m_new = jnp.maximum(m_prev, jnp.max(sc, axis=-1, keepdims=True))
            alpha = jnp.exp(m_prev - m_new)
            p = jnp.exp(sc - _lanes(m_new, blk))
            l_ref[s] = alpha * l_ref[s] + jnp.sum(p, axis=-1, keepdims=True)
            acc_ref[s] = _lanes(alpha, hv) * acc_ref[s] + _dot(p.astype(BF16), vb)
            m_ref[s] = m_new

    @pl.when(j < i)
    def _():
        step(False)

    @pl.when(j == i)
    def _():
        step(True)
        lam = _diff_lambda(lam_ref, lam_init)
        o = acc_ref[0] / _lanes(l_ref[0], hv) - lam * (acc_ref[1] / _lanes(l_ref[1], hv))
        o_ref[...] = (_rms(o, sg_ref[...], eps) * (1.0 - lam_init)).astype(o_ref.dtype)


def _attn_prompt(q, k, v, lam_p, subln_g, n_seq, seq_len, *, lam_init, blk):
    hv = 2 * LANES
    nh = q.shape[1] // hv
    nb = seq_len // blk
    pairs = [(i, j) for i in range(nb) for j in range(i + 1)]
    qi = jnp.asarray([p[0] for p in pairs], jnp.int32)
    kj = jnp.asarray([p[1] for p in pairs], jnp.int32)
    return pl.pallas_call(
        functools.partial(_attn_prompt_kernel, blk=blk, lam_init=lam_init, eps=SUBLN_EPS),
        out_shape=jax.ShapeDtypeStruct((n_seq * seq_len, nh * hv), BF16),
        grid_spec=pltpu.PrefetchScalarGridSpec(
            num_scalar_prefetch=2,
            grid=(n_seq, nh, len(pairs)),
            in_specs=[
                pl.BlockSpec(lam_p.shape, lambda b, h, p, qi, kj: (0, 0)),
                pl.BlockSpec((1, hv), lambda b, h, p, qi, kj: (0, 0)),
                pl.BlockSpec((blk, hv), lambda b, h, p, qi, kj: (b * nb + qi[p], h)),
                pl.BlockSpec((blk, hv), lambda b, h, p, qi, kj: (b * nb + kj[p], h)),
                pl.BlockSpec((blk, hv), lambda b, h, p, qi, kj: (b * nb + kj[p], h)),
            ],
            out_specs=pl.BlockSpec((blk, hv), lambda b, h, p, qi, kj: (b * nb + qi[p], h)),
            scratch_shapes=[pltpu.VMEM((2, blk, LANES), F32), pltpu.VMEM((2, blk, LANES), F32),
                            pltpu.VMEM((2, blk, hv), F32)],
        ),
        compiler_params=_params(),
    )(qi, kj, lam_p, subln_g.reshape(1, hv), q, k, v)


def _attn_sample_kernel(pt_ref, lam_ref, sg_ref, q_ref, kn_ref, vn_ref, *rest, pages, tq, nh, lam_init, eps):
    k_refs = rest[:pages]
    v_refs = rest[pages:3 * pages]
    o_ref, m_ref, l_ref, acc_ref = rest[3 * pages:]
    g = pl.program_id(1)
    hv = 2 * LANES
    rows = 2 * tq
    n_sub = 2 * nh

    @pl.when(g == 0)
    def _():
        m_ref[...] = jnp.full_like(m_ref, NEG_BIG)
        l_ref[...] = jnp.zeros_like(l_ref)
        acc_ref[...] = jnp.zeros_like(acc_ref)

    first = lax.broadcasted_iota(jnp.int32, (rows, 1), 0) < tq

    def scores(k_sub):
        out = []
        for h in range(nh):
            qa = q_ref[h * rows:(h + 1) * rows, :]
            out.append(jnp.where(first, _dot_nt(qa, k_sub(2 * h)), _dot_nt(qa, k_sub(2 * h + 1))))
        return jnp.concatenate(out, axis=0)

    def update(sc, v_heads):
        nkeys = sc.shape[1] // len(v_heads)
        m_prev = m_ref[...]
        m_new = jnp.maximum(m_prev, jnp.max(sc, axis=-1, keepdims=True))
        alpha = jnp.exp(m_prev - m_new)
        p = jnp.exp(sc - m_new)
        l_ref[...] = alpha * l_ref[...] + jnp.sum(p, axis=-1, keepdims=True)
        pb = p.astype(BF16)
        pv = []
        for h in range(nh):
            parts = [_dot(pb[h * rows:(h + 1) * rows, b * nkeys:(b + 1) * nkeys], v_head(h))
                     for b, v_head in enumerate(v_heads)]
            pv.append(functools.reduce(lambda x, y: x + y, parts))
        acc_ref[...] = alpha * acc_ref[...] + jnp.concatenate(pv, axis=0)
        m_ref[...] = m_new

    psz = k_refs[0].shape[0] // n_sub

    def k_sub_of(r):
        return lambda j: r[pl.ds(j, psz, stride=n_sub), :].astype(BF16)

    def v_head_of(lo, hi):
        return lambda h: jnp.concatenate([lo[pl.ds(h, psz, stride=nh), :], hi[pl.ds(h, psz, stride=nh), :]],
                                         axis=1).astype(BF16)

    update(jnp.concatenate([scores(k_sub_of(k_refs[pg])) for pg in range(pages)], axis=1),
           [v_head_of(v_refs[2 * pg], v_refs[2 * pg + 1]) for pg in range(pages)])

    @pl.when(g == pl.num_programs(1) - 1)
    def _():
        pad = jnp.zeros((LANES - tq, kn_ref.shape[1]), F32)
        kb = jnp.concatenate([kn_ref[...], pad], axis=0).astype(BF16)
        vb = jnp.concatenate([vn_ref[...], pad], axis=0).astype(BF16)
        sc = scores(lambda j: kb[:, j * LANES:(j + 1) * LANES])
        qpos = lax.broadcasted_iota(jnp.int32, sc.shape, 0) % tq
        kpos = lax.broadcasted_iota(jnp.int32, sc.shape, 1)
        update(jnp.where(kpos <= qpos, sc, NEG_BIG), [lambda h: vb[:, h * hv:(h + 1) * hv]])
        lam = _diff_lambda(lam_ref, lam_init)
        attn = acc_ref[...] / l_ref[...]
        sg = sg_ref[...]
        for h in range(nh):
            o = attn[h * rows:h * rows + tq] - lam * attn[h * rows + tq:(h + 1) * rows]
            o_ref[:, h * hv:(h + 1) * hv] = _rms(o, sg, eps) * (1.0 - lam_init)


def _attn_sample(q16, k_new, v_new, cache_k, cache_v, page_table, lam_p, subln_g, tq, *, lam_init, pages):
    n_seq, n_pages = page_table.shape
    hv = 2 * LANES
    width = k_new.shape[1]
    nh = width // hv
    rows = nh * 2 * tq

    def page_spec(cache, pg, lane_blk):
        return pl.BlockSpec((None, cache.shape[1], LANES),
                            lambda b, g, pt: (pt[b * n_pages + g * pages + pg], 0, lane_blk))

    return pl.pallas_call(
        functools.partial(_attn_sample_kernel, pages=pages, tq=tq, nh=nh, lam_init=lam_init, eps=SUBLN_EPS),
        out_shape=jax.ShapeDtypeStruct((n_seq, tq, width), F32),
        grid_spec=pltpu.PrefetchScalarGridSpec(
            num_scalar_prefetch=1,
            grid=(n_seq, n_pages // pages),
            in_specs=[
                pl.BlockSpec(lam_p.shape, lambda b, g, pt: (0, 0)),
                pl.BlockSpec((1, hv), lambda b, g, pt: (0, 0)),
                pl.BlockSpec((None, rows, LANES), lambda b, g, pt: (b, 0, 0)),
                pl.BlockSpec((tq, width), lambda b, g, pt: (b, 0)),
                pl.BlockSpec((tq, width), lambda b, g, pt: (b, 0)),
            ] + [page_spec(cache_k, pg, 0) for pg in range(pages)]
              + [page_spec(cache_v, pg, half) for pg in range(pages) for half in range(2)],
            out_specs=pl.BlockSpec((None, tq, width), lambda b, g, pt: (b, 0, 0)),
            scratch_shapes=[pltpu.VMEM((rows, 1), F32), pltpu.VMEM((rows, 1), F32), pltpu.VMEM((rows, hv), F32)],
        ),
        compiler_params=_params(),
    )(page_table.reshape(-1), lam_p, subln_g.reshape(1, hv), q16, k_new, v_new,
      *([cache_k] * pages), *([cache_v] * (2 * pages)))


def _moe_route_kernel(y_ref, g_ref, wr_ref, hn_ref, r_ref, *, eps, n_exp):
    h = _rms(y_ref[...], g_ref[...], eps)
    hn_ref[...] = h
    h1, h2, h3 = _split3(h)
    w1, w2, w3 = _split3(wr_ref[...])
    logits = (_dot(h1, w1) + (_dot(h1, w2) + _dot(h2, w1)) + (_dot(h1, w3) + _dot(h2, w2) + _dot(h3, w1)))
    lane = lax.broadcasted_iota(jnp.int32, logits.shape, 1)
    logits = jnp.where(lane < n_exp, logits, NEG_BIG)
    m1 = jnp.max(logits, axis=-1, keepdims=True)
    i1 = jnp.min(jnp.where(logits == m1, lane, LANES), axis=-1, keepdims=True)
    rest = jnp.where(lane == i1, NEG_BIG, logits)
    m2 = jnp.max(rest, axis=-1, keepdims=True)
    i2 = jnp.min(jnp.where(rest == m2, lane, LANES), axis=-1, keepdims=True)
    e2 = jnp.exp(m2 - m1)
    w_top1 = 1.0 / (1.0 + e2)
    w_top2 = e2 / (1.0 + e2)
    r_ref[...] = jnp.where(lane == 0, i1.astype(F32),
                           jnp.where(lane == 1, i2.astype(F32),
                                     jnp.where(lane == 2, w_top1, jnp.where(lane == 3, w_top2, 0.0))))


def _moe_route(y, g, w_router, *, bm):
    t, d = y.shape
    n_exp = w_router.shape[1]
    wr = jnp.pad(w_router, ((0, 0), (0, LANES - n_exp)))
    return pl.pallas_call(
        functools.partial(_moe_route_kernel, eps=NORM_EPS, n_exp=n_exp),
        out_shape=(jax.ShapeDtypeStruct((t, d), F32), jax.ShapeDtypeStruct((t, LANES), F32)),
        grid=(t // bm,),
        in_specs=[
            pl.BlockSpec((bm, d), lambda i: (i, 0)),
            pl.BlockSpec((1, d), lambda i: (0, 0)),
            pl.BlockSpec((d, LANES), lambda i: (0, 0)),
        ],
        out_specs=(pl.BlockSpec((bm, d), lambda i: (i, 0)), pl.BlockSpec((bm, LANES), lambda i: (i, 0))),
        compiler_params=_params(),
    )(y, g.reshape(1, d), wr)


def _row_copy(src_hbm, dst, src_row, dst_row, sem):
    return pltpu.make_async_copy(src_hbm.at[pl.ds(src_row, 1)], dst.at[pl.ds(dst_row, 1)], sem)


def _moe_expert_kernel(te_ref, nr_ref, tok_ref, hn_ref, gate_ref, wg_ref, wu_ref, wd_ref, o_ref, xin_ref, xb_ref, sem,
                       *, bm, sub):
    i = pl.program_id(0)
    f = pl.program_id(1)
    nrows = nr_ref[i]

    def issue(tile):
        def body(r, carry):
            _row_copy(hn_ref, xin_ref, tok_ref[tile * bm + r], r, sem).start()
            return carry
        lax.fori_loop(0, nr_ref[tile], body, 0)

    @pl.when(f == 0)
    def _():
        @pl.when(i == 0)
        def _():
            xin_ref[...] = jnp.zeros_like(xin_ref)
            issue(0)

        def drain(r, carry):
            _row_copy(hn_ref, xin_ref, 0, r, sem).wait()
            return carry
        lax.fori_loop(0, nrows, drain, 0)
        row = lax.broadcasted_iota(jnp.int32, (bm, 1), 0)
        xb_ref[...] = jnp.where(row < nrows, xin_ref[...], 0.0).astype(BF16)
        o_ref[...] = jnp.zeros_like(o_ref)

        @pl.when(i + 1 < pl.num_programs(0))
        def _():
            issue(i + 1)

    for m_rows in range(sub, bm + 1, sub):
        @pl.when((nrows > m_rows - sub) & (nrows <= m_rows))
        def _(m_rows=m_rows):
            x = xb_ref[0:m_rows]
            gate = _dot(x, wg_ref[...].astype(BF16))
            up = _dot(x, wu_ref[...].astype(BF16))
            act = (_silu(gate) * up).astype(BF16)
            o_ref[0:m_rows] += _dot(act, wd_ref[...].astype(BF16))

    @pl.when(f == pl.num_programs(1) - 1)
    def _():
        o_ref[...] = o_ref[...] * gate_ref[...]


def _moe_experts(hn, token_sorted, gate_sorted, tile_expert, tile_rows, w_gu, w_down, *, bm, bf, sub):
    d = hn.shape[1]
    n_tiles = tile_expert.shape[0]
    dff = w_down.shape[1]
    nf = dff // bf

    def fblk(i, f, nr):
        return jnp.where(nr[i] > 0, f, nf - 1)

    return pl.pallas_call(
        functools.partial(_moe_expert_kernel, bm=bm, sub=sub),
        out_shape=jax.ShapeDtypeStruct((n_tiles * bm, d), F32),
        grid_spec=pltpu.PrefetchScalarGridSpec(
            num_scalar_prefetch=3,
            grid=(n_tiles, nf),
            in_specs=[
                pl.BlockSpec(memory_space=pl.ANY),
                pl.BlockSpec((bm, 1), lambda i, f, te, nr, tok: (i, 0)),
                pl.BlockSpec((None, d, bf), lambda i, f, te, nr, tok: (te[i], 0, fblk(i, f, nr))),
                pl.BlockSpec((None, d, bf), lambda i, f, te, nr, tok: (te[i], 0, nf + fblk(i, f, nr))),
                pl.BlockSpec((None, bf, d), lambda i, f, te, nr, tok: (te[i], fblk(i, f, nr), 0)),
            ],
            out_specs=pl.BlockSpec((bm, d), lambda i, f, te, nr, tok: (i, 0)),
            scratch_shapes=[pltpu.VMEM((bm, d), F32), pltpu.VMEM((bm, d), BF16), pltpu.SemaphoreType.DMA],
        ),
        compiler_params=_params(),
    )(tile_expert, tile_rows, token_sorted, hn, gate_sorted.reshape(n_tiles * bm, 1), w_gu, w_gu, w_down)


def _moe_combine_kernel(pos_ref, y_ref, g_ref, ys_ref, o_ref, buf_ref, sem, *, bt, eps):
    def issue(r, carry):
        _row_copy(ys_ref, buf_ref.at[0], pos_ref[0, 0, 2 * r], r, sem).start()
        _row_copy(ys_ref, buf_ref.at[1], pos_ref[0, 0, 2 * r + 1], r, sem).start()
        return carry

    lax.fori_loop(0, bt, issue, 0)

    def drain(r, carry):
        _row_copy(ys_ref, buf_ref.at[0], 0, r, sem).wait()
        _row_copy(ys_ref, buf_ref.at[1], 0, r, sem).wait()
        return carry

    lax.fori_loop(0, bt, drain, 0)
    z = y_ref[...] + (buf_ref[0] + buf_ref[1])
    o_ref[...] = _rms(z, g_ref[...], eps)


def _moe_combine_norm(y, ys, pos, g, row0, nrows, *, bt):
    d = y.shape[1]
    steps = nrows // bt
    rb = row0 // bt
    return pl.pallas_call(
        functools.partial(_moe_combine_kernel, bt=bt, eps=NORM_EPS),
        out_shape=jax.ShapeDtypeStruct((nrows, d), F32),
        grid=(steps,),
        in_specs=[
            pl.BlockSpec((1, 1, 2 * bt), lambda i: (i, 0, 0), memory_space=pltpu.SMEM),
            pl.BlockSpec((bt, d), lambda i: (rb + i, 0)),
            pl.BlockSpec((1, d), lambda i: (0, 0)),
            pl.BlockSpec(memory_space=pl.ANY),
        ],
        out_specs=pl.BlockSpec((bt, d), lambda i: (i, 0)),
        scratch_shapes=[pltpu.VMEM((2, bt, d), F32), pltpu.SemaphoreType.DMA],
        compiler_params=_params(),
    )(pos[2 * row0:2 * (row0 + nrows)].reshape(steps, 1, 2 * bt), y, g.reshape(1, d), ys)


def _moe_plan(route, n_exp, bm, n_tiles):
    t = route.shape[0]
    experts = route[:, 0:2].astype(jnp.int32).reshape(-1)
    gate_bits = lax.bitcast_convert_type(route[:, 2:4].reshape(-1), jnp.int32)
    onehot = (experts[:, None] == jnp.arange(n_exp, dtype=jnp.int32)[None, :]).astype(jnp.int32)
    rank = jnp.sum((jnp.cumsum(onehot, axis=0) - onehot) * onehot, axis=1)
    counts = jnp.sum(onehot, axis=0)
    tiles_per = (counts + bm - 1) // bm
    rows_per = (((counts + jnp.maximum(tiles_per, 1) - 1) // jnp.maximum(tiles_per, 1)) + 7) // 8 * 8
    rows_per = jnp.maximum(rows_per, 8)
    tile_end = jnp.cumsum(tiles_per)
    tile_start = tile_end - tiles_per
    dest = (tile_start[experts] + rank // rows_per[experts]) * bm + rank % rows_per[experts]
    slot_vals = jnp.stack([jnp.arange(2 * t, dtype=jnp.int32) // 2, gate_bits], axis=1)
    sorted_vals = jnp.zeros((n_tiles * bm, 2), jnp.int32).at[dest].set(slot_vals)
    token_sorted = sorted_vals[:, 0]
    gate_sorted = lax.bitcast_convert_type(sorted_vals[:, 1], F32)
    n_used = tile_end[-1]
    tiles = jnp.arange(n_tiles, dtype=jnp.int32)
    tile_expert = jnp.sum((jnp.minimum(tiles, n_used - 1)[:, None] >= tile_end[None, :]).astype(jnp.int32), axis=1)
    local = tiles - tile_start[tile_expert]
    tile_rows = jnp.clip(counts[tile_expert] - local * rows_per[tile_expert], 0, rows_per[tile_expert])
    tile_rows = jnp.where(tiles < n_used, tile_rows, 0)
    return token_sorted, gate_sorted, tile_expert.astype(jnp.int32), tile_rows.astype(jnp.int32), dest


def _moe_block_norm(y, g_ffn, w_router, w_gu, w_down, g_out, splits, *, bm_route, bm, bf, sub, bt):
    t = y.shape[0]
    n_exp = w_router.shape[1]
    hn, route = _moe_route(y, g_ffn, w_router, bm=bm_route)
    n_tiles = (2 * t) // bm + n_exp
    token_sorted, gate_sorted, tile_expert, tile_rows, dest = _moe_plan(route, n_exp, bm, n_tiles)
    ys = _moe_experts(hn, token_sorted, gate_sorted, tile_expert, tile_rows, w_gu, w_down, bm=bm, bf=bf, sub=sub)
    return [_moe_combine_norm(y, ys, dest, g_out, row0, nrows, bt=bt) for row0, nrows in splits]


def _rope_tables(positions, half, reps):
    inv = ROPE_THETA ** (-np.arange(half, dtype=np.float64) / half)
    ang = np.asarray(positions, np.float64)[:, None] * inv[None, :]
    cos, sin = np.cos(ang), np.sin(ang)
    cos2 = np.tile(np.concatenate([cos, cos], axis=1), (reps, 1))
    sin2 = np.tile(np.concatenate([-sin, sin], axis=1), (reps, 1))
    return jnp.asarray(cos2, F32), jnp.asarray(sin2, F32)


def _step(x_prompt, x_sample, cache_k, cache_v, state_hgrn, page_table, norm_mix_g, norm_ffn_g, norm_out_g,
          hg_w_in, hg_lb_logits, hg_norm_g, hg_w_out, da_w_in, da_lambda, da_subln_g, da_w_out,
          ffn_w_gu, ffn_w_down, moe_w_router, moe_w_gu, moe_w_down, *, cfg):
    nb_p, seq, d = x_prompt.shape
    nb_s, tq, _ = x_sample.shape
    tp, ts = nb_p * seq, nb_s * tq
    t = tp + ts
    past = page_table.shape[1] * cache_k.shape[2]
    dh = cache_k.shape[-1]
    n_sub = cache_k.shape[-2]
    nh_attn = n_sub // 2
    bm, bn, bmp = cfg["bm"], cfg["bn"], cfg["bm_prompt"]
    y = jnp.concatenate([x_prompt.reshape(tp, d), x_sample.reshape(ts, d)], axis=0)

    hk = hg_w_in.shape[2] // 4
    (proj,) = _norm_proj(y, norm_mix_g[0], hg_w_in[0], 0, 4 * hk, 0, t, eps=NORM_EPS, bm=bm, bn=bn, out_dtypes=(F32,))
    o_p, st_p = _hgrn_prompt(proj, hg_lb_logits, hg_norm_g[0], nb_p, seq, layer=0, c=cfg["hg_chunk"], hb=cfg["hg_hb"])
    o_s, st_s = _hgrn_sample(proj, tp, state_hgrn[0], hg_lb_logits, hg_norm_g[0], tq, layer=0,
                             nb=cfg["hg_nb"], hb=cfg["hg_hb"])
    y = _matmul_residual(jnp.concatenate([o_p, o_s], axis=0), hg_w_out[0], y, bm=bm, bn=bn)
    y = _ffn_dense(y, norm_ffn_g[0], ffn_w_gu[0], ffn_w_down[0], eps=NORM_EPS, bm=cfg["ffn_bm"], bf=cfg["ffn_bf"])

    lam_init = 0.8 - 0.6 * math.exp(-0.3 * 1)
    qk = n_sub * dh
    g1 = norm_mix_g[1]
    w_in = da_w_in[0]
    cos_p, sin_p = _rope_tables(np.arange(seq), dh // 2, 1)
    cos_s, sin_s = _rope_tables(past + np.arange(tq), dh // 2, nb_s)
    proj_p = functools.partial(_norm_proj, y, g1, w_in, row0=0, nrows=tp, eps=NORM_EPS, bm=bmp, bn=bn)
    proj_s = functools.partial(_norm_proj, y, g1, w_in, row0=tp, nrows=ts, eps=NORM_EPS, bm=ts, bn=bn)
    q_scale = dh ** -0.5
    (q16_p,) = proj_p(col0=0, ncols=qk, out_dtypes=(BF16,), rope=(cos_p, sin_p, q_scale))
    (q16_s,) = proj_s(col0=0, ncols=qk, out_dtypes=(BF16,), rope=(cos_s, sin_s, q_scale))
    k32_p, k16_p = proj_p(col0=qk, ncols=qk, out_dtypes=(F32, BF16), rope=(cos_p, sin_p, 1.0))
    (k32_s,) = proj_s(col0=qk, ncols=qk, out_dtypes=(F32,), rope=(cos_s, sin_s, 1.0))
    v32_p, v16_p = proj_p(col0=2 * qk, ncols=qk, out_dtypes=(F32, BF16))
    (v32_s,) = proj_s(col0=2 * qk, ncols=qk, out_dtypes=(F32,))
    a_p = _attn_prompt(q16_p, k16_p, v16_p, da_lambda[0], da_subln_g[0], nb_p, seq, lam_init=lam_init,
                       blk=cfg["attn_blk"])
    q_s = q16_s.reshape(nb_s, tq, nh_attn, 2, dh).transpose(0, 2, 3, 1, 4).reshape(nb_s, nh_attn * 2 * tq, dh)
    ck = cache_k[0].reshape(cache_k.shape[1], cache_k.shape[2] * n_sub, dh)
    cv = cache_v[0].reshape(cache_v.shape[1], cache_v.shape[2] * nh_attn, 2 * dh)
    a_s = _attn_sample(q_s, k32_s, v32_s, ck, cv, page_table, da_lambda[0], da_subln_g[0], tq,
                       lam_init=lam_init, pages=cfg["attn_pages"])
    attn = jnp.concatenate([a_p, a_s.reshape(ts, qk).astype(BF16)], axis=0)
    y = _matmul_residual(attn, da_w_out[0], y, bm=bm, bn=bn)
    y_p, y_s = _moe_block_norm(y, norm_ffn_g[1], moe_w_router[0], moe_w_gu[0], moe_w_down[0], norm_out_g,
                               [(0, tp), (tp, ts)], bm_route=cfg["route_bm"], bm=cfg["moe_bm"], bf=cfg["moe_bf"],
                               sub=cfg["moe_sub"], bt=cfg["combine_bt"])

    return (y_p.reshape(nb_p, seq, d), y_s.reshape(nb_s, tq, d),
            k32_p.reshape(1, nb_p, seq, n_sub, dh), v32_p.reshape(1, nb_p, seq, nh_attn, 2 * dh),
            k32_s.reshape(1, nb_s, tq, n_sub, dh), v32_s.reshape(1, nb_s, tq, nh_attn, 2 * dh),
            st_p[None], st_s[None])


_CFG = dict(bm=1056, bn=512, bm_prompt=1024, hg_chunk=128, hg_hb=4, hg_nb=8, ffn_bm=768, ffn_bf=256, attn_blk=512,
            attn_pages=4, route_bm=264, moe_bm=768, moe_bf=512, moe_sub=256, combine_bt=256)


def kernel(x_prompt, x_sample, cache_k, cache_v, state_hgrn, page_table, norm_mix_g, norm_ffn_g, norm_out_g,
           hg_w_in, hg_lb_logits, hg_norm_g, hg_w_out, da_w_in, da_lambda, da_subln_g, da_w_out,
           ffn_w_gu, ffn_w_down, moe_w_router, moe_w_gu, moe_w_down):
    return _step(x_prompt, x_sample, cache_k, cache_v, state_hgrn, page_table, norm_mix_g, norm_ffn_g, norm_out_g,
                 hg_w_in, hg_lb_logits, hg_norm_g, hg_w_out, da_w_in, da_lambda, da_subln_g, da_w_out,
                 ffn_w_gu, ffn_w_down, moe_w_router, moe_w_gu, moe_w_down, cfg=_CFG)
```

```python
import functools
import math

import numpy as np
import jax
import jax.numpy as jnp
from jax import lax
from jax.experimental import pallas as pl
from jax.experimental.pallas import tpu as pltpu

F32 = jnp.float32
BF16 = jnp.bfloat16

LANES = 128
SUBLANES = 8
NORM_EPS = 1e-6
SUBLN_EPS = 1e-5
ROPE_THETA = 10000.0
NEG_BIG = -1e30
VMEM_LIMIT = 56 * 1024 * 1024


def _params(**kw):
    return pltpu.CompilerParams(vmem_limit_bytes=VMEM_LIMIT, **kw)


def _dot(a, b):
    return jnp.dot(a, b, preferred_element_type=F32)


def _dot_nt(a, b):
    return lax.dot_general(a, b, (((1,), (1,)), ((), ())), preferred_element_type=F32)


def _dot_tn(a, b):
    return lax.dot_general(a, b, (((0,), (0,)), ((), ())), preferred_element_type=F32)


def _sigmoid(x):
    return 1.0 / (1.0 + jnp.exp(-x))


def _silu(x):
    return x * _sigmoid(x)


def _rms(x, g, eps):
    return x * lax.rsqrt(jnp.mean(x * x, axis=-1, keepdims=True) + eps) * g


def _split2(x):
    hi = x.astype(BF16)
    return hi, (x - hi.astype(F32)).astype(BF16)


def _split3(x):
    hi = x.astype(BF16)
    r1 = x - hi.astype(F32)
    mid = r1.astype(BF16)
    lo = (r1 - mid.astype(F32)).astype(BF16)
    return hi, mid, lo


def _norm_proj_kernel(x_ref, g_ref, w_ref, *rest, eps, rope_scale, n_out):
    o_refs, h_ref = rest[-1 - n_out:-1], rest[-1]

    @pl.when(pl.program_id(1) == 0)
    def _():
        h_ref[...] = _rms(x_ref[...], g_ref[...], eps).astype(BF16)

    acc = _dot(h_ref[...], w_ref[...].astype(BF16))
    if rope_scale is None:
        for o_ref in o_refs:
            o_ref[...] = acc.astype(o_ref.dtype)
    else:
        cos = rest[0][...]
        sin = rest[1][...]
        for c in range(acc.shape[1] // LANES):
            sl = slice(c * LANES, (c + 1) * LANES)
            xh = acc[:, sl]
            r = (xh * cos + pltpu.roll(xh, LANES // 2, 1) * sin) * rope_scale
            for o_ref in o_refs:
                o_ref[:, sl] = r.astype(o_ref.dtype)


def _norm_proj(x, g, w, col0, ncols, row0, nrows, *, eps, bm, bn, out_dtypes, rope=None):
    d = x.shape[1]
    cb, rb = col0 // bn, row0 // bm
    in_specs = [
        pl.BlockSpec((bm, d), lambda i, j: (rb + i, 0)),
        pl.BlockSpec((1, d), lambda i, j: (0, 0)),
        pl.BlockSpec((d, bn), lambda i, j: (0, cb + j)),
    ]
    args = [x, g.reshape(1, d), w]
    scale = None
    if rope is not None:
        cos, sin, scale = rope
        period = cos.shape[0] // bm
        in_specs += [pl.BlockSpec((bm, LANES), lambda i, j: (i % period, 0))] * 2
        args += [cos, sin]
    return pl.pallas_call(
        functools.partial(_norm_proj_kernel, eps=eps, rope_scale=scale, n_out=len(out_dtypes)),
        out_shape=tuple(jax.ShapeDtypeStruct((nrows, ncols), dt) for dt in out_dtypes),
        grid=(nrows // bm, ncols // bn),
        in_specs=in_specs,
        out_specs=tuple(pl.BlockSpec((bm, bn), lambda i, j: (i, j)) for _ in out_dtypes),
        scratch_shapes=[pltpu.VMEM((bm, d), BF16)],
        compiler_params=_params(),
    )(*args)


def _matmul_residual_kernel(a_ref, w_ref, r_ref, o_ref):
    o_ref[...] = r_ref[...] + _dot(a_ref[...], w_ref[...].astype(BF16))


def _matmul_residual(a, w, res, *, bm, bn):
    t, k = a.shape
    n = w.shape[1]
    return pl.pallas_call(
        _matmul_residual_kernel,
        out_shape=jax.ShapeDtypeStruct((t, n), F32),
        grid=(t // bm, n // bn),
        in_specs=[
            pl.BlockSpec((bm, k), lambda i, j: (i, 0)),
            pl.BlockSpec((k, bn), lambda i, j: (0, j)),
            pl.BlockSpec((bm, bn), lambda i, j: (i, j)),
        ],
        out_specs=pl.BlockSpec((bm, bn), lambda i, j: (i, j)),
        compiler_params=_params(),
    )(a, w, res)


def _ffn_kernel(x_ref, g_ref, wg_ref, wu_ref, wd_ref, o_ref, h_ref, *, eps):
    @pl.when(pl.program_id(1) == 0)
    def _():
        x = x_ref[...]
        h_ref[...] = _rms(x, g_ref[...], eps).astype(BF16)
        o_ref[...] = x

    h = h_ref[...]
    gate = _dot(h, wg_ref[...].astype(BF16))
    up = _dot(h, wu_ref[...].astype(BF16))
    act = (_silu(gate) * up).astype(BF16)
    o_ref[...] += _dot(act, wd_ref[...].astype(BF16))


def _ffn_dense(y, g, w_gu, w_down, *, eps, bm, bf):
    t, d = y.shape
    dff = w_down.shape[0]
    nf = dff // bf
    return pl.pallas_call(
        functools.partial(_ffn_kernel, eps=eps),
        out_shape=jax.ShapeDtypeStruct((t, d), F32),
        grid=(t // bm, nf),
        in_specs=[
            pl.BlockSpec((bm, d), lambda i, f: (i, 0)),
            pl.BlockSpec((1, d), lambda i, f: (0, 0)),
            pl.BlockSpec((d, bf), lambda i, f: (0, f)),
            pl.BlockSpec((d, bf), lambda i, f: (0, nf + f)),
            pl.BlockSpec((bf, d), lambda i, f: (f, 0)),
        ],
        out_specs=pl.BlockSpec((bm, d), lambda i, f: (i, 0)),
        scratch_shapes=[pltpu.VMEM((bm, d), BF16)],
        compiler_params=_params(),
    )(y, g.reshape(1, d), w_gu, w_gu, w_down)


def _hgrn_levels(seg):
    levels = []
    m = seg // 2
    while m >= 1:
        levels.append(m)
        m //= 2
    return levels


def _hgrn_consts(c, seg):
    t = np.arange(c)[:, None]
    p = np.arange(c)[None, :]
    same = (t // seg) == (p // seg)
    blocks = [(p <= t) & same]
    masks = []
    for m in _hgrn_levels(seg):
        r = (t // (2 * m)) * (2 * m) + m - 1
        upper = (t % (2 * m)) >= m
        if 2 * m < SUBLANES:
            blocks.append(np.where(upper, (p > r) & (p <= t), (p > t) & (p <= r)))
        masks.append(((t // (2 * m)) == (p // (2 * m))) & upper & ((p % (2 * m)) < m))
    masks.append(t == p)
    w = np.concatenate(blocks, axis=0).astype(np.float32)
    return jnp.asarray(w, BF16), jnp.asarray(np.stack(masks).astype(np.float32))


def _hgrn_lower_bound(logits, layer):
    e = jnp.exp(logits - jnp.max(logits, axis=0, keepdims=True))
    sm = e / jnp.sum(e, axis=0, keepdims=True)
    return jnp.sum(sm[: layer + 1], axis=0, keepdims=True)


def _hgrn_gates(qp, fp, lb, kdim):
    q = _silu(qp) * (kdim ** -0.5)
    f = lb + (1.0 - lb) * _sigmoid(fp)
    return q, jnp.log(f), 1.0 - f


def _block_row(g, blk, off):
    parts = [jnp.broadcast_to(g[b * blk + off:b * blk + off + 1], (blk, g.shape[1])) for b in range(g.shape[0] // blk)]
    return parts[0] if len(parts) == 1 else jnp.concatenate(parts, axis=0)


def _hgrn_intra(q, k, lg, w, m_ref, seg):
    c = q.shape[0]
    levels = _hgrn_levels(seg)
    hi, lo = _split2(lg)
    e = _dot(w, hi) + _dot(w, lo)
    g = e[0:c]
    a = m_ref[len(levels)] * _dot_nt(q.astype(BF16), k.astype(BF16))
    short = 0
    for li, m in enumerate(levels):
        if 2 * m >= SUBLANES:
            xm = jnp.exp(-jnp.abs(g - _block_row(g, 2 * m, m - 1)))
        else:
            short += 1
            xm = jnp.exp(e[short * c:(short + 1) * c])
        a = a + m_ref[li] * _dot_nt((q * xm).astype(BF16), (k * xm).astype(BF16))
    return jnp.exp(g), jnp.exp(-jnp.abs(g - _block_row(g, seg, seg - 1))), a


def _hgrn_prompt_kernel(q_ref, f_ref, i_ref, g_ref, lb_ref, ng_ref, w_ref, m_ref, o_ref, s_ref, st_ref,
                        *, hb, c, layer, eps):
    ci = pl.program_id(2)

    @pl.when(ci == 0)
    def _():
        st_ref[...] = jnp.zeros_like(st_ref)

    w = w_ref[...]
    for hh in range(hb):
        sl = slice(hh * LANES, (hh + 1) * LANES)
        lb = _hgrn_lower_bound(lb_ref[:, sl], layer)
        q, lg, k = _hgrn_gates(q_ref[:, sl], f_ref[:, sl], lb, LANES)
        xq, xk, a = _hgrn_intra(q, k, lg, w, m_ref, c)
        vb = i_ref[:, sl].astype(BF16)
        st = st_ref[hh]
        o = _dot_nt((q * xq).astype(BF16), st.astype(BF16)) + _dot(a.astype(BF16), vb)
        st_ref[hh] = st * xq[c - 1:c] + _dot_tn(vb, (k * xk).astype(BF16))
        o_ref[:, sl] = (_rms(o, ng_ref[...], eps) * _silu(g_ref[:, sl])).astype(o_ref.dtype)

    @pl.when(ci == pl.num_programs(2) - 1)
    def _():
        for hh in range(hb):
            s_ref[0, hh] = st_ref[hh].T


def _hgrn_prompt(proj, lb_logits, norm_g, n_seq, seq_len, *, layer, c, hb):
    hk = proj.shape[1] // 4
    nh = hk // LANES
    nc = seq_len // c
    ngrp = nh // hb
    w, masks = _hgrn_consts(c, c)
    bw = hb * LANES

    def col(part):
        return pl.BlockSpec((c, bw), lambda b, h, ci: (b * nc + ci, part * ngrp + h))

    return pl.pallas_call(
        functools.partial(_hgrn_prompt_kernel, hb=hb, c=c, layer=layer, eps=NORM_EPS),
        out_shape=(jax.ShapeDtypeStruct((n_seq * seq_len, hk), BF16),
                   jax.ShapeDtypeStruct((n_seq, nh, LANES, LANES), F32)),
        grid=(n_seq, ngrp, nc),
        in_specs=[
            col(0), col(1), col(2), col(3),
            pl.BlockSpec((lb_logits.shape[0], bw), lambda b, h, ci: (0, h)),
            pl.BlockSpec((1, LANES), lambda b, h, ci: (0, 0)),
            pl.BlockSpec(w.shape, lambda b, h, ci: (0, 0)),
            pl.BlockSpec(masks.shape, lambda b, h, ci: (0, 0, 0)),
        ],
        out_specs=(pl.BlockSpec((c, bw), lambda b, h, ci: (b * nc + ci, h)),
                   pl.BlockSpec((1, hb, LANES, LANES), lambda b, h, ci: (b, h, 0, 0))),
        scratch_shapes=[pltpu.VMEM((hb, LANES, LANES), F32)],
        compiler_params=_params(),
    )(proj, proj, proj, proj, lb_logits, norm_g.reshape(1, LANES), w, masks)


def _hgrn_sample_kernel(q_ref, f_ref, i_ref, g_ref, lb_ref, ng_ref, w_ref, m_ref, s0_ref, o_ref, s_ref,
                        *, hb, nb, tq, layer, eps):
    c = nb * tq
    w = w_ref[...]
    seq_of_row = lax.broadcasted_iota(jnp.int32, (c, 1), 0) // tq
    for hh in range(hb):
        sl = slice(hh * LANES, (hh + 1) * LANES)
        lb = _hgrn_lower_bound(lb_ref[:, sl], layer)
        q, lg, k = _hgrn_gates(q_ref[:, sl], f_ref[:, sl], lb, LANES)
        xq, xk, a = _hgrn_intra(q, k, lg, w, m_ref, tq)
        vb = i_ref[:, sl].astype(BF16)
        q0 = (q * xq).astype(BF16)
        kl = k * xk
        o = _dot(a.astype(BF16), vb)
        for b in range(nb):
            mine = seq_of_row == b
            st = s0_ref[b, hh].T
            o = o + jnp.where(mine, _dot_nt(q0, st.astype(BF16)), 0.0)
            r_last = (b + 1) * tq - 1
            klb = jnp.where(mine, kl, 0.0).astype(BF16)
            s_ref[b, hh] = (st * xq[r_last:r_last + 1] + _dot_tn(vb, klb)).T
        o_ref[:, sl] = (_rms(o, ng_ref[...], eps) * _silu(g_ref[:, sl])).astype(o_ref.dtype)


def _hgrn_sample(proj, row0, state, lb_logits, norm_g, tq, *, layer, nb, hb):
    n_seq, nh = state.shape[0], state.shape[1]
    hk = proj.shape[1] // 4
    ngrp = nh // hb
    c = nb * tq
    rb0 = row0 // c
    w, masks = _hgrn_consts(c, tq)
    bw = hb * LANES

    def col(part):
        return pl.BlockSpec((c, bw), lambda sb, h: (rb0 + sb, part * ngrp + h))

    return pl.pallas_call(
        functools.partial(_hgrn_sample_kernel, hb=hb, nb=nb, tq=tq, layer=layer, eps=NORM_EPS),
        out_shape=(jax.ShapeDtypeStruct((n_seq * tq, hk), BF16),
                   jax.ShapeDtypeStruct(state.shape, F32)),
        grid=(n_seq // nb, ngrp),
        in_specs=[
            col(0), col(1), col(2), col(3),
            pl.BlockSpec((lb_logits.shape[0], bw), lambda sb, h: (0, h)),
            pl.BlockSpec((1, LANES), lambda sb, h: (0, 0)),
            pl.BlockSpec(w.shape, lambda sb, h: (0, 0)),
            pl.BlockSpec(masks.shape, lambda sb, h: (0, 0, 0)),
            pl.BlockSpec((nb, hb, LANES, LANES), lambda sb, h: (sb, h, 0, 0)),
        ],
        out_specs=(pl.BlockSpec((c, bw), lambda sb, h: (sb, h)),
                   pl.BlockSpec((nb, hb, LANES, LANES), lambda sb, h: (sb, h, 0, 0))),
        compiler_params=_params(),
    )(proj, proj, proj, proj, lb_logits, norm_g.reshape(1, LANES), w, masks, state)


def _diff_lambda(lam_ref, lam_init):
    lp = lam_ref[...]
    s1 = jnp.sum(lp[0:1] * lp[1:2], axis=-1, keepdims=True)
    s2 = jnp.sum(lp[2:3] * lp[3:4], axis=-1, keepdims=True)
    return jnp.exp(s1) - jnp.exp(s2) + lam_init


def _lanes(x, width):
    return x if width == LANES else jnp.concatenate([x] * (width // LANES), axis=1)


def _attn_prompt_kernel(qi_ref, kj_ref, lam_ref, sg_ref, q_ref, k_ref, v_ref, o_ref, m_ref, l_ref, acc_ref,
                        *, blk, lam_init, eps):
    pair = pl.program_id(2)
    i = qi_ref[pair]
    j = kj_ref[pair]
    hv = 2 * LANES

    @pl.when(j == 0)
    def _():
        m_ref[...] = jnp.full_like(m_ref, NEG_BIG)
        l_ref[...] = jnp.zeros_like(l_ref)
        acc_ref[...] = jnp.zeros_like(acc_ref)

    def step(diagonal):
        vb = v_ref[...]
        for s in range(2):
            sl = slice(s * LANES, (s + 1) * LANES)
            sc = _dot_nt(q_ref[:, sl], k_ref[:, sl])
            if diagonal:
                row = lax.broadcasted_iota(jnp.int32, (blk, blk), 0)
                colv = lax.broadcasted_iota(jnp.int32, (blk, blk), 1)
                sc = jnp.where(colv <= row, sc, NEG_BIG)
            m_prev = m_ref[s]
            m_new = jnp.maximum(m_prev, jnp.max(sc, axis=-1, keepdims=True))
            alpha = jnp.exp(m_prev - m_new)
            p = jnp.exp(sc - _lanes(m_new, blk))
            l_ref[s] = alpha * l_ref[s] + jnp.sum(p, axis=-1, keepdims=True)
            acc_ref[s] = _lanes(alpha, hv) * acc_ref[s] + _dot(p.astype(BF16), vb)
            m_ref[s] = m_new

    @pl.when(j < i)
    def _():
        step(False)

    @pl.when(j == i)
    def _():
        step(True)
        lam = _diff_lambda(lam_ref, lam_init)
        o = acc_ref[0] / _lanes(l_ref[0], hv) - lam * (acc_ref[1] / _lanes(l_ref[1], hv))
        o_ref[...] = (_rms(o, sg_ref[...], eps) * (1.0 - lam_init)).astype(o_ref.dtype)


def _attn_prompt(q, k, v, lam_p, subln_g, n_seq, seq_len, *, lam_init, blk):
    hv = 2 * LANES
    nh = q.shape[1] // hv
    nb = seq_len // blk
    pairs = [(i, j) for i in range(nb) for j in range(i + 1)]
    qi = jnp.asarray([p[0] for p in pairs], jnp.int32)
    kj = jnp.asarray([p[1] for p in pairs], jnp.int32)
    return pl.pallas_call(
        functools.partial(_attn_prompt_kernel, blk=blk, lam_init=lam_init, eps=SUBLN_EPS),
        out_shape=jax.ShapeDtypeStruct((n_seq * seq_len, nh * hv), BF16),
        grid_spec=pltpu.PrefetchScalarGridSpec(
            num_scalar_prefetch=2,
            grid=(n_seq, nh, len(pairs)),
            in_specs=[
                pl.BlockSpec(lam_p.shape, lambda b, h, p, qi, kj: (0, 0)),
                pl.BlockSpec((1, hv), lambda b, h, p, qi, kj: (0, 0)),
                pl.BlockSpec((blk, hv), lambda b, h, p, qi, kj: (b * nb + qi[p], h)),
                pl.BlockSpec((blk, hv), lambda b, h, p, qi, kj: (b * nb + kj[p], h)),
                pl.BlockSpec((blk, hv), lambda b, h, p, qi, kj: (b * nb + kj[p], h)),
            ],
            out_specs=pl.BlockSpec((blk, hv), lambda b, h, p, qi, kj: (b * nb + qi[p], h)),
            scratch_shapes=[pltpu.VMEM((2, blk, LANES), F32), pltpu.VMEM((2, blk, LANES), F32),
                            pltpu.VMEM((2, blk, hv), F32)],
        ),
        compiler_params=_params(),
    )(qi, kj, lam_p, subln_g.reshape(1, hv), q, k, v)


def _attn_sample_kernel(pt_ref, lam_ref, sg_ref, q_ref, kn_ref, vn_ref, *rest, pages, tq, nh, lam_init, eps):
    k_refs = rest[:pages]
    v_refs = rest[pages:3 * pages]
    o_ref, m_ref, l_ref, acc_ref = rest[3 * pages:]
    g = pl.program_id(1)
    hv = 2 * LANES
    rows = 2 * tq
    n_sub = 2 * nh

    @pl.when(g == 0)
    def _():
        m_ref[...] = jnp.full_like(m_ref, NEG_BIG)
        l_ref[...] = jnp.zeros_like(l_ref)
        acc_ref[...] = jnp.zeros_like(acc_ref)

    first = lax.broadcasted_iota(jnp.int32, (rows, 1), 0) < tq

    def scores(k_sub):
        out = []
        for h in range(nh):
            qa = q_ref[h * rows:(h + 1) * rows, :]
            out.append(jnp.where(first, _dot_nt(qa, k_sub(2 * h)), _dot_nt(qa, k_sub(2 * h + 1))))
        return jnp.concatenate(out, axis=0)

    def update(sc, v_heads):
        nkeys = sc.shape[1] // len(v_heads)
        m_prev = m_ref[...]
        m_new = jnp.maximum(m_prev, jnp.max(sc, axis=-1, keepdims=True))
        alpha = jnp.exp(m_prev - m_new)
        p = jnp.exp(sc - m_new)
        l_ref[...] = alpha * l_ref[...] + jnp.sum(p, axis=-1, keepdims=True)
        pb = p.astype(BF16)
        pv = []
        for h in range(nh):
            parts = [_dot(pb[h * rows:(h + 1) * rows, b * nkeys:(b + 1) * nkeys], v_head(h))
                     for b, v_head in enumerate(v_heads)]
            pv.append(functools.reduce(lambda x, y: x + y, parts))
        acc_ref[...] = alpha * acc_ref[...] + jnp.concatenate(pv, axis=0)
        m_ref[...] = m_new

    psz = k_refs[0].shape[0] // n_sub

    def k_sub_of(r):
        return lambda j: r[pl.ds(j, psz, stride=n_sub), :].astype(BF16)

    def v_head_of(lo, hi):
        return lambda h: jnp.concatenate([lo[pl.ds(h, psz, stride=nh), :], hi[pl.ds(h, psz, stride=nh), :]],
                                         axis=1).astype(BF16)

    update(jnp.concatenate([scores(k_sub_of(k_refs[pg])) for pg in range(pages)], axis=1),
           [v_head_of(v_refs[2 * pg], v_refs[2 * pg + 1]) for pg in range(pages)])

    @pl.when(g == pl.num_programs(1) - 1)
    def _():
        pad = jnp.zeros((LANES - tq, kn_ref.shape[1]), F32)
        kb = jnp.concatenate([kn_ref[...], pad], axis=0).astype(BF16)
        vb = jnp.concatenate([vn_ref[...], pad], axis=0).astype(BF16)
        sc = scores(lambda j: kb[:, j * LANES:(j + 1) * LANES])
        qpos = lax.broadcasted_iota(jnp.int32, sc.shape, 0) % tq
        kpos = lax.broadcasted_iota(jnp.int32, sc.shape, 1)
        update(jnp.where(kpos <= qpos, sc, NEG_BIG), [lambda h: vb[:, h * hv:(h + 1) * hv]])
        lam = _diff_lambda(lam_ref, lam_init)
        attn = acc_ref[...] / l_ref[...]
        sg = sg_ref[...]
        for h in range(nh):
            o = attn[h * rows:h * rows + tq] - lam * attn[h * rows + tq:(h + 1) * rows]
            o_ref[:, h * hv:(h + 1) * hv] = _rms(o, sg, eps) * (1.0 - lam_init)


def _attn_sample(q16, k_new, v_new, cache_k, cache_v, page_table, lam_p, subln_g, tq, *, lam_init, pages):
    n_seq, n_pages = page_table.shape
    hv = 2 * LANES
    width = k_new.shape[1]
    nh = width // hv
    rows = nh * 2 * tq

    def page_spec(cache, pg, lane_blk):
        return pl.BlockSpec((None, cache.shape[1], LANES),
                            lambda b, g, pt: (pt[b * n_pages + g * pages + pg], 0, lane_blk))

    return pl.pallas_call(
        functools.partial(_attn_sample_kernel, pages=pages, tq=tq, nh=nh, lam_init=lam_init, eps=SUBLN_EPS),
        out_shape=jax.ShapeDtypeStruct((n_seq, tq, width), F32),
        grid_spec=pltpu.PrefetchScalarGridSpec(
            num_scalar_prefetch=1,
            grid=(n_seq, n_pages // pages),
            in_specs=[
                pl.BlockSpec(lam_p.shape, lambda b, g, pt: (0, 0)),
                pl.BlockSpec((1, hv), lambda b, g, pt: (0, 0)),
                pl.BlockSpec((None, rows, LANES), lambda b, g, pt: (b, 0, 0)),
                pl.BlockSpec((tq, width), lambda b, g, pt: (b, 0)),
                pl.BlockSpec((tq, width), lambda b, g, pt: (b, 0)),
            ] + [page_spec(cache_k, pg, 0) for pg in range(pages)]
              + [page_spec(cache_v, pg, half) for pg in range(pages) for half in range(2)],
            out_specs=pl.BlockSpec((None, tq, width), lambda b, g, pt: (b, 0, 0)),
            scratch_shapes=[pltpu.VMEM((rows, 1), F32), pltpu.VMEM((rows, 1), F32), pltpu.VMEM((rows, hv), F32)],
        ),
        compiler_params=_params(),
    )(page_table.reshape(-1), lam_p, subln_g.reshape(1, hv), q16, k_new, v_new,
      *([cache_k] * pages), *([cache_v] * (2 * pages)))


def _moe_route_kernel(y_ref, g_ref, wr_ref, hn_ref, r_ref, *, eps, n_exp):
    h = _rms(y_ref[...], g_ref[...], eps)
    hn_ref[...] = h
    h1, h2, h3 = _split3(h)
    w1, w2, w3 = _split3(wr_ref[...])
    logits = (_dot(h1, w1) + (_dot(h1, w2) + _dot(h2, w1)) + (_dot(h1, w3) + _dot(h2, w2) + _dot(h3, w1)))
    lane = lax.broadcasted_iota(jnp.int32, logits.shape, 1)
    logits = jnp.where(lane < n_exp, logits, NEG_BIG)
    m1 = jnp.max(logits, axis=-1, keepdims=True)
    i1 = jnp.min(jnp.where(logits == m1, lane, LANES), axis=-1, keepdims=True)
    rest = jnp.where(lane == i1, NEG_BIG, logits)
    m2 = jnp.max(rest, axis=-1, keepdims=True)
    i2 = jnp.min(jnp.where(rest == m2, lane, LANES), axis=-1, keepdims=True)
    e2 = jnp.exp(m2 - m1)
    w_top1 = 1.0 / (1.0 + e2)
    w_top2 = e2 / (1.0 + e2)
    r_ref[...] = jnp.where(lane == 0, i1.astype(F32),
                           jnp.where(lane == 1, i2.astype(F32),
                                     jnp.where(lane == 2, w_top1, jnp.where(lane == 3, w_top2, 0.0))))


def _moe_route(y, g, w_router, *, bm):
    t, d = y.shape
    n_exp = w_router.shape[1]
    wr = jnp.pad(w_router, ((0, 0), (0, LANES - n_exp)))
    return pl.pallas_call(
        functools.partial(_moe_route_kernel, eps=NORM_EPS, n_exp=n_exp),
        out_shape=(jax.ShapeDtypeStruct((t, d), F32), jax.ShapeDtypeStruct((t, LANES), F32)),
        grid=(t // bm,),
        in_specs=[
            pl.BlockSpec((bm, d), lambda i: (i, 0)),
            pl.BlockSpec((1, d), lambda i: (0, 0)),
            pl.BlockSpec((d, LANES), lambda i: (0, 0)),
        ],
        out_specs=(pl.BlockSpec((bm, d), lambda i: (i, 0)), pl.BlockSpec((bm, LANES), lambda i: (i, 0))),
        compiler_params=_params(),
    )(y, g.reshape(1, d), wr)


def _row_copy(src_hbm, dst, src_row, dst_row, sem):
    return pltpu.make_async_copy(src_hbm.at[pl.ds(src_row, 1)], dst.at[pl.ds(dst_row, 1)], sem)


def _moe_expert_kernel(te_ref, nr_ref, tok_ref, hn_ref, gate_ref, wg_ref, wu_ref, wd_ref, o_ref, xin_ref, xb_ref, sem,
                       *, bm, sub):
    i = pl.program_id(0)
    f = pl.program_id(1)
    nrows = nr_ref[i]

    def groups(tile):
        return (nr_ref[tile] + SUBLANES - 1) // SUBLANES

    def issue(tile):
        def body(grp, carry):
            for u in range(SUBLANES):
                r = grp * SUBLANES + u
                _row_copy(hn_ref, xin_ref, tok_ref[tile * bm + r], r, sem).start()
            return carry
        lax.fori_loop(0, groups(tile), body, 0)

    @pl.when(f == 0)
    def _():
        @pl.when(i == 0)
        def _():
            xin_ref[...] = jnp.zeros_like(xin_ref)
            issue(0)

        def drain(grp, carry):
            for u in range(SUBLANES):
                _row_copy(hn_ref, xin_ref, 0, grp * SUBLANES + u, sem).wait()
            return carry
        lax.fori_loop(0, groups(i), drain, 0)
        row = lax.broadcasted_iota(jnp.int32, (bm, 1), 0)
        xb_ref[...] = jnp.where(row < nrows, xin_ref[...], 0.0).astype(BF16)
        o_ref[...] = jnp.zeros_like(o_ref)

        @pl.when(i + 1 < pl.num_programs(0))
        def _():
            issue(i + 1)

    def sub_block(r, carry):
        rows = pl.ds(pl.multiple_of(r * sub, sub), sub)
        x = xb_ref[rows, :]
        gate = _dot(x, wg_ref[...].astype(BF16))
        up = _dot(x, wu_ref[...].astype(BF16))
        act = (_silu(gate) * up).astype(BF16)
        o_ref[rows, :] += _dot(act, wd_ref[...].astype(BF16))
        return carry

    lax.fori_loop(0, (nrows + sub - 1) // sub, sub_block, 0)

    @pl.when(f == pl.num_programs(1) - 1)
    def _():
        o_ref[...] = o_ref[...] * gate_ref[...]


def _moe_experts(hn, token_sorted, gate_sorted, tile_expert, tile_rows, w_gu, w_down, *, bm, bf, sub):
    d = hn.shape[1]
    n_tiles = tile_expert.shape[0]
    dff = w_down.shape[1]
    nf = dff // bf

    def fblk(i, f, nr):
        return jnp.where(nr[i] > 0, f, nf - 1)

    return pl.pallas_call(
        functools.partial(_moe_expert_kernel, bm=bm, sub=sub),
        out_shape=jax.ShapeDtypeStruct((n_tiles * bm, d), F32),
        grid_spec=pltpu.PrefetchScalarGridSpec(
            num_scalar_prefetch=3,
            grid=(n_tiles, nf),
            in_specs=[
                pl.BlockSpec(memory_space=pl.ANY),
                pl.BlockSpec((bm, 1), lambda i, f, te, nr, tok: (i, 0)),
                pl.BlockSpec((None, d, bf), lambda i, f, te, nr, tok: (te[i], 0, fblk(i, f, nr))),
                pl.BlockSpec((None, d, bf), lambda i, f, te, nr, tok: (te[i], 0, nf + fblk(i, f, nr))),
                pl.BlockSpec((None, bf, d), lambda i, f, te, nr, tok: (te[i], fblk(i, f, nr), 0)),
            ],
            out_specs=pl.BlockSpec((bm, d), lambda i, f, te, nr, tok: (i, 0)),
            scratch_shapes=[pltpu.VMEM((bm, d), F32), pltpu.VMEM((bm, d), BF16), pltpu.SemaphoreType.DMA],
        ),
        compiler_params=_params(),
    )(tile_expert, tile_rows, token_sorted, hn, gate_sorted.reshape(n_tiles * bm, 1), w_gu, w_gu, w_down)


def _moe_combine_kernel(pos_ref, y_ref, g_ref, ys_ref, o_ref, buf_ref, sem, *, bt, eps):
    def issue(r, carry):
        _row_copy(ys_ref, buf_ref.at[0], pos_ref[0, 0, 2 * r], r, sem).start()
        _row_copy(ys_ref, buf_ref.at[1], pos_ref[0, 0, 2 * r + 1], r, sem).start()
        return carry

    lax.fori_loop(0, bt, issue, 0, unroll=SUBLANES)

    def drain(r, carry):
        _row_copy(ys_ref, buf_ref.at[0], 0, r, sem).wait()
        _row_copy(ys_ref, buf_ref.at[1], 0, r, sem).wait()
        return carry

    lax.fori_loop(0, bt, drain, 0, unroll=SUBLANES)
    z = y_ref[...] + (buf_ref[0] + buf_ref[1])
    o_ref[...] = _rms(z, g_ref[...], eps)


def _moe_combine_norm(y, ys, pos, g, row0, nrows, *, bt):
    d = y.shape[1]
    steps = nrows // bt
    rb = row0 // bt
    return pl.pallas_call(
        functools.partial(_moe_combine_kernel, bt=bt, eps=NORM_EPS),
        out_shape=jax.ShapeDtypeStruct((nrows, d), F32),
        grid=(steps,),
        in_specs=[
            pl.BlockSpec((1, 1, 2 * bt), lambda i: (i, 0, 0), memory_space=pltpu.SMEM),
            pl.BlockSpec((bt, d), lambda i: (rb + i, 0)),
            pl.BlockSpec((1, d), lambda i: (0, 0)),
            pl.BlockSpec(memory_space=pl.ANY),
        ],
        out_specs=pl.BlockSpec((bt, d), lambda i: (i, 0)),
        scratch_shapes=[pltpu.VMEM((2, bt, d), F32), pltpu.SemaphoreType.DMA],
        compiler_params=_params(),
    )(pos[2 * row0:2 * (row0 + nrows)].reshape(steps, 1, 2 * bt), y, g.reshape(1, d), ys)


def _exclusive_cumsum_rows(x):
    n, k = x.shape
    nblk = n // LANES
    xb = x.reshape(nblk, LANES, k).astype(F32)
    strict = jnp.asarray(np.tril(np.ones((LANES, LANES), np.float32), -1))
    within = jnp.einsum("ij,bjk->bik", strict, xb, precision=lax.Precision.HIGHEST)
    totals = jnp.sum(xb, axis=1)
    before = jnp.cumsum(totals, axis=0) - totals
    return (within + before[:, None, :]).astype(jnp.int32).reshape(n, k)


def _moe_plan(route, n_exp, bm, n_tiles):
    t = route.shape[0]
    experts = route[:, 0:2].astype(jnp.int32).reshape(-1)
    gate_bits = lax.bitcast_convert_type(route[:, 2:4].reshape(-1), jnp.int32)
    onehot = (experts[:, None] == jnp.arange(n_exp, dtype=jnp.int32)[None, :]).astype(jnp.int32)
    rank = jnp.sum(_exclusive_cumsum_rows(onehot) * onehot, axis=1)
    counts = jnp.sum(onehot, axis=0)
    tiles_per = (counts + bm - 1) // bm
    rows_per = (((counts + jnp.maximum(tiles_per, 1) - 1) // jnp.maximum(tiles_per, 1)) + 7) // 8 * 8
    rows_per = jnp.maximum(rows_per, 8)
    tile_end = jnp.cumsum(tiles_per)
    tile_start = tile_end - tiles_per
    dest = (tile_start[experts] + rank // rows_per[experts]) * bm + rank % rows_per[experts]
    slot_vals = jnp.stack([jnp.arange(2 * t, dtype=jnp.int32) // 2, gate_bits], axis=1)
    sorted_vals = jnp.zeros((n_tiles * bm, 2), jnp.int32).at[dest].set(slot_vals)
    token_sorted = sorted_vals[:, 0]
    gate_sorted = lax.bitcast_convert_type(sorted_vals[:, 1], F32)
    n_used = tile_end[-1]
    tiles = jnp.arange(n_tiles, dtype=jnp.int32)
    tile_expert = jnp.sum((jnp.minimum(tiles, n_used - 1)[:, None] >= tile_end[None, :]).astype(jnp.int32), axis=1)
    local = tiles - tile_start[tile_expert]
    tile_rows = jnp.clip(counts[tile_expert] - local * rows_per[tile_expert], 0, rows_per[tile_expert])
    tile_rows = jnp.where(tiles < n_used, tile_rows, 0)
    return token_sorted, gate_sorted, tile_expert.astype(jnp.int32), tile_rows.astype(jnp.int32), dest


def _moe_block_norm(y, g_ffn, w_router, w_gu, w_down, g_out, splits, *, bm_route, bm, bf, sub, bt):
    t = y.shape[0]
    n_exp = w_router.shape[1]
    hn, route = _moe_route(y, g_ffn, w_router, bm=bm_route)
    n_tiles = (2 * t) // bm + n_exp
    token_sorted, gate_sorted, tile_expert, tile_rows, dest = _moe_plan(route, n_exp, bm, n_tiles)
    ys = _moe_experts(hn, token_sorted, gate_sorted, tile_expert, tile_rows, w_gu, w_down, bm=bm, bf=bf, sub=sub)
    return [_moe_combine_norm(y, ys, dest, g_out, row0, nrows, bt=bt) for row0, nrows in splits]


def _rope_tables(positions, half, reps):
    inv = ROPE_THETA ** (-np.arange(half, dtype=np.float64) / half)
    ang = np.asarray(positions, np.float64)[:, None] * inv[None, :]
    cos, sin = np.cos(ang), np.sin(ang)
    cos2 = np.tile(np.concatenate([cos, cos], axis=1), (reps, 1))
    sin2 = np.tile(np.concatenate([-sin, sin], axis=1), (reps, 1))
    return jnp.asarray(cos2, F32), jnp.asarray(sin2, F32)


def _step(x_prompt, x_sample, cache_k, cache_v, state_hgrn, page_table, norm_mix_g, norm_ffn_g, norm_out_g,
          hg_w_in, hg_lb_logits, hg_norm_g, hg_w_out, da_w_in, da_lambda, da_subln_g, da_w_out,
          ffn_w_gu, ffn_w_down, moe_w_router, moe_w_gu, moe_w_down, *, cfg):
    nb_p, seq, d = x_prompt.shape
    nb_s, tq, _ = x_sample.shape
    tp, ts = nb_p * seq, nb_s * tq
    t = tp + ts
    past = page_table.shape[1] * cache_k.shape[2]
    dh = cache_k.shape[-1]
    n_sub = cache_k.shape[-2]
    nh_attn = n_sub // 2
    bm, bn, bmp = cfg["bm"], cfg["bn"], cfg["bm_prompt"]
    y = jnp.concatenate([x_prompt.reshape(tp, d), x_sample.reshape(ts, d)], axis=0)

    hk = hg_w_in.shape[2] // 4
    (proj,) = _norm_proj(y, norm_mix_g[0], hg_w_in[0], 0, 4 * hk, 0, t, eps=NORM_EPS, bm=bm, bn=cfg["bn_wide"],
                         out_dtypes=(F32,))
    o_p, st_p = _hgrn_prompt(proj, hg_lb_logits, hg_norm_g[0], nb_p, seq, layer=0, c=cfg["hg_chunk"], hb=cfg["hg_hb"])
    o_s, st_s = _hgrn_sample(proj, tp, state_hgrn[0], hg_lb_logits, hg_norm_g[0], tq, layer=0,
                             nb=cfg["hg_nb"], hb=cfg["hg_hb"])
    y = _matmul_residual(jnp.concatenate([o_p, o_s], axis=0), hg_w_out[0], y, bm=bm, bn=bn)
    y = _ffn_dense(y, norm_ffn_g[0], ffn_w_gu[0], ffn_w_down[0], eps=NORM_EPS, bm=cfg["ffn_bm"], bf=cfg["ffn_bf"])

    lam_init = 0.8 - 0.6 * math.exp(-0.3 * 1)
    qk = n_sub * dh
    g1 = norm_mix_g[1]
    w_in = da_w_in[0]
    cos_p, sin_p = _rope_tables(np.arange(seq), dh // 2, 1)
    cos_s, sin_s = _rope_tables(past + np.arange(tq), dh // 2, nb_s)
    proj_p = functools.partial(_norm_proj, y, g1, w_in, row0=0, nrows=tp, eps=NORM_EPS, bm=bmp, bn=bn)
    proj_s = functools.partial(_norm_proj, y, g1, w_in, row0=tp, nrows=ts, eps=NORM_EPS, bm=ts, bn=bn)
    q_scale = dh ** -0.5
    (q16_p,) = proj_p(col0=0, ncols=qk, out_dtypes=(BF16,), rope=(cos_p, sin_p, q_scale))
    (q16_s,) = proj_s(col0=0, ncols=qk, out_dtypes=(BF16,), rope=(cos_s, sin_s, q_scale))
    k32_p, k16_p = proj_p(col0=qk, ncols=qk, out_dtypes=(F32, BF16), rope=(cos_p, sin_p, 1.0))
    (k32_s,) = proj_s(col0=qk, ncols=qk, out_dtypes=(F32,), rope=(cos_s, sin_s, 1.0))
    v32_p, v16_p = proj_p(col0=2 * qk, ncols=qk, out_dtypes=(F32, BF16))
    (v32_s,) = proj_s(col0=2 * qk, ncols=qk, out_dtypes=(F32,))
    a_p = _attn_prompt(q16_p, k16_p, v16_p, da_lambda[0], da_subln_g[0], nb_p, seq, lam_init=lam_init,
                       blk=cfg["attn_blk"])
    q_s = q16_s.reshape(nb_s, tq, nh_attn, 2, dh).transpose(0, 2, 3, 1, 4).reshape(nb_s, nh_attn * 2 * tq, dh)
    ck = cache_k[0].reshape(cache_k.shape[1], cache_k.shape[2] * n_sub, dh)
    cv = cache_v[0].reshape(cache_v.shape[1], cache_v.shape[2] * nh_attn, 2 * dh)
    a_s = _attn_sample(q_s, k32_s, v32_s, ck, cv, page_table, da_lambda[0], da_subln_g[0], tq,
                       lam_init=lam_init, pages=cfg["attn_pages"])
    attn = jnp.concatenate([a_p, a_s.reshape(ts, qk).astype(BF16)], axis=0)
    y = _matmul_residual(attn, da_w_out[0], y, bm=bm, bn=bn)
    y_p, y_s = _moe_block_norm(y, norm_ffn_g[1], moe_w_router[0], moe_w_gu[0], moe_w_down[0], norm_out_g,
                               [(0, tp), (tp, ts)], bm_route=cfg["route_bm"], bm=cfg["moe_bm"], bf=cfg["moe_bf"],
                               sub=cfg["moe_sub"], bt=cfg["combine_bt"])

    return (y_p.reshape(nb_p, seq, d), y_s.reshape(nb_s, tq, d),
            k32_p.reshape(1, nb_p, seq, n_sub, dh), v32_p.reshape(1, nb_p, seq, nh_attn, 2 * dh),
            k32_s.reshape(1, nb_s, tq, n_sub, dh), v32_s.reshape(1, nb_s, tq, nh_attn, 2 * dh),
            st_p[None], st_s[None])


_CFG = dict(bm=1056, bn=512, bn_wide=1024, bm_prompt=1024, hg_chunk=128, hg_hb=8, hg_nb=8, ffn_bm=704, ffn_bf=512,
            attn_blk=512, attn_pages=8, route_bm=264, moe_bm=768, moe_bf=512, moe_sub=256, combine_bt=256)


def kernel(x_prompt, x_sample, cache_k, cache_v, state_hgrn, page_table, norm_mix_g, norm_ffn_g, norm_out_g,
           hg_w_in, hg_lb_logits, hg_norm_g, hg_w_out, da_w_in, da_lambda, da_subln_g, da_w_out,
           ffn_w_gu, ffn_w_down, moe_w_router, moe_w_gu, moe_w_down):
    return _step(x_prompt, x_sample, cache_k, cache_v, state_hgrn, page_table, norm_mix_g, norm_ffn_g, norm_out_g,
                 hg_w_in, hg_lb_logits, hg_norm_g, hg_w_out, da_w_in, da_lambda, da_subln_g, da_w_out,
                 ffn_w_gu, ffn_w_down, moe_w_router, moe_w_gu, moe_w_down, cfg=_CFG)
```

```python
import functools
import math

import numpy as np
import jax
import jax.numpy as jnp
from jax import lax
from jax.experimental import pallas as pl
from jax.experimental.pallas import tpu as pltpu

F32 = jnp.float32
BF16 = jnp.bfloat16

LANES = 128
SUBLANES = 8
NORM_EPS = 1e-6
SUBLN_EPS = 1e-5
ROPE_THETA = 10000.0
NEG_BIG = -1e30
VMEM_LIMIT = 56 * 1024 * 1024


def _params(**kw):
    return pltpu.CompilerParams(vmem_limit_bytes=VMEM_LIMIT, **kw)


def _dot(a, b):
    return jnp.dot(a, b, preferred_element_type=F32)


def _dot_nt(a, b):
    return lax.dot_general(a, b, (((1,), (1,)), ((), ())), preferred_element_type=F32)


def _dot_tn(a, b):
    return lax.dot_general(a, b, (((0,), (0,)), ((), ())), preferred_element_type=F32)


def _sigmoid(x):
    return 1.0 / (1.0 + jnp.exp(-x))


def _silu(x):
    return x * _sigmoid(x)


def _rms(x, g, eps):
    return x * lax.rsqrt(jnp.mean(x * x, axis=-1, keepdims=True) + eps) * g


def _split2(x):
    hi = x.astype(BF16)
    return hi, (x - hi.astype(F32)).astype(BF16)


def _split3(x):
    hi = x.astype(BF16)
    r1 = x - hi.astype(F32)
    mid = r1.astype(BF16)
    lo = (r1 - mid.astype(F32)).astype(BF16)
    return hi, mid, lo


def _norm_proj_kernel(x_ref, g_ref, w_ref, *rest, eps, rope_scale, n_out):
    o_refs, h_ref = rest[-1 - n_out:-1], rest[-1]

    @pl.when(pl.program_id(1) == 0)
    def _():
        h_ref[...] = _rms(x_ref[...], g_ref[...], eps).astype(BF16)

    acc = _dot(h_ref[...], w_ref[...].astype(BF16))
    if rope_scale is None:
        for o_ref in o_refs:
            o_ref[...] = acc.astype(o_ref.dtype)
    else:
        cos = rest[0][...]
        sin = rest[1][...]
        for c in range(acc.shape[1] // LANES):
            sl = slice(c * LANES, (c + 1) * LANES)
            xh = acc[:, sl]
            r = (xh * cos + pltpu.roll(xh, LANES // 2, 1) * sin) * rope_scale
            for o_ref in o_refs:
                o_ref[:, sl] = r.astype(o_ref.dtype)


def _norm_proj(x, g, w, col0, ncols, row0, nrows, *, eps, bm, bn, out_dtypes, rope=None):
    d = x.shape[1]
    cb, rb = col0 // bn, row0 // bm
    in_specs = [
        pl.BlockSpec((bm, d), lambda i, j: (rb + i, 0)),
        pl.BlockSpec((1, d), lambda i, j: (0, 0)),
        pl.BlockSpec((d, bn), lambda i, j: (0, cb + j)),
    ]
    args = [x, g.reshape(1, d), w]
    scale = None
    if rope is not None:
        cos, sin, scale = rope
        period = cos.shape[0] // bm
        in_specs += [pl.BlockSpec((bm, LANES), lambda i, j: (i % period, 0))] * 2
        args += [cos, sin]
    return pl.pallas_call(
        functools.partial(_norm_proj_kernel, eps=eps, rope_scale=scale, n_out=len(out_dtypes)),
        out_shape=tuple(jax.ShapeDtypeStruct((nrows, ncols), dt) for dt in out_dtypes),
        grid=(nrows // bm, ncols // bn),
        in_specs=in_specs,
        out_specs=tuple(pl.BlockSpec((bm, bn), lambda i, j: (i, j)) for _ in out_dtypes),
        scratch_shapes=[pltpu.VMEM((bm, d), BF16)],
        compiler_params=_params(),
    )(*args)


def _matmul_residual_kernel(a_ref, w_ref, r_ref, o_ref):
    o_ref[...] = r_ref[...] + _dot(a_ref[...], w_ref[...].astype(BF16))


def _matmul_residual(a, w, res, *, bm, bn):
    t, k = a.shape
    n = w.shape[1]
    return pl.pallas_call(
        _matmul_residual_kernel,
        out_shape=jax.ShapeDtypeStruct((t, n), F32),
        grid=(t // bm, n // bn),
        in_specs=[
            pl.BlockSpec((bm, k), lambda i, j: (i, 0)),
            pl.BlockSpec((k, bn), lambda i, j: (0, j)),
            pl.BlockSpec((bm, bn), lambda i, j: (i, j)),
        ],
        out_specs=pl.BlockSpec((bm, bn), lambda i, j: (i, j)),
        compiler_params=_params(),
    )(a, w, res)


def _ffn_kernel(x_ref, g_ref, wg_ref, wu_ref, wd_ref, o_ref, h_ref, *, eps):
    @pl.when(pl.program_id(1) == 0)
    def _():
        x = x_ref[...]
        h_ref[...] = _rms(x, g_ref[...], eps).astype(BF16)
        o_ref[...] = x

    h = h_ref[...]
    gate = _dot(h, wg_ref[...].astype(BF16))
    up = _dot(h, wu_ref[...].astype(BF16))
    act = (_silu(gate) * up).astype(BF16)
    o_ref[...] += _dot(act, wd_ref[...].astype(BF16))


def _ffn_dense(y, g, w_gu, w_down, *, eps, bm, bf):
    t, d = y.shape
    dff = w_down.shape[0]
    nf = dff // bf
    return pl.pallas_call(
        functools.partial(_ffn_kernel, eps=eps),
        out_shape=jax.ShapeDtypeStruct((t, d), F32),
        grid=(t // bm, nf),
        in_specs=[
            pl.BlockSpec((bm, d), lambda i, f: (i, 0)),
            pl.BlockSpec((1, d), lambda i, f: (0, 0)),
            pl.BlockSpec((d, bf), lambda i, f: (0, f)),
            pl.BlockSpec((d, bf), lambda i, f: (0, nf + f)),
            pl.BlockSpec((bf, d), lambda i, f: (f, 0)),
        ],
        out_specs=pl.BlockSpec((bm, d), lambda i, f: (i, 0)),
        scratch_shapes=[pltpu.VMEM((bm, d), BF16)],
        compiler_params=_params(),
    )(y, g.reshape(1, d), w_gu, w_gu, w_down)


def _hgrn_levels(seg):
    levels = []
    m = seg // 2
    while m >= 1:
        levels.append(m)
        m //= 2
    return levels


def _hgrn_consts(c, seg):
    t = np.arange(c)[:, None]
    p = np.arange(c)[None, :]
    same = (t // seg) == (p // seg)
    blocks = [(p <= t) & same]
    masks = []
    for m in _hgrn_levels(seg):
        r = (t // (2 * m)) * (2 * m) + m - 1
        upper = (t % (2 * m)) >= m
        if 2 * m < SUBLANES:
            blocks.append(np.where(upper, (p > r) & (p <= t), (p > t) & (p <= r)))
        masks.append(((t // (2 * m)) == (p // (2 * m))) & upper & ((p % (2 * m)) < m))
    masks.append(t == p)
    w = np.concatenate(blocks, axis=0).astype(np.float32)
    return jnp.asarray(w, BF16), jnp.asarray(np.stack(masks).astype(np.float32))


def _hgrn_lower_bound(logits, layer):
    e = jnp.exp(logits - jnp.max(logits, axis=0, keepdims=True))
    sm = e / jnp.sum(e, axis=0, keepdims=True)
    return jnp.sum(sm[: layer + 1], axis=0, keepdims=True)


def _hgrn_gates(qp, fp, lb, kdim):
    q = _silu(qp) * (kdim ** -0.5)
    f = lb + (1.0 - lb) * _sigmoid(fp)
    return q, jnp.log(f), 1.0 - f


def _block_row(g, blk, off):
    parts = [jnp.broadcast_to(g[b * blk + off:b * blk + off + 1], (blk, g.shape[1])) for b in range(g.shape[0] // blk)]
    return parts[0] if len(parts) == 1 else jnp.concatenate(parts, axis=0)


def _hgrn_intra(q, k, lg, w, m_ref, seg):
    c = q.shape[0]
    levels = _hgrn_levels(seg)
    hi, lo = _split2(lg)
    e = _dot(w, hi) + _dot(w, lo)
    g = e[0:c]
    a = m_ref[len(levels)] * _dot_nt(q.astype(BF16), k.astype(BF16))
    short = 0
    for li, m in enumerate(levels):
        if 2 * m >= SUBLANES:
            xm = jnp.exp(-jnp.abs(g - _block_row(g, 2 * m, m - 1)))
        else:
            short += 1
            xm = jnp.exp(e[short * c:(short + 1) * c])
        a = a + m_ref[li] * _dot_nt((q * xm).astype(BF16), (k * xm).astype(BF16))
    return jnp.exp(g), jnp.exp(-jnp.abs(g - _block_row(g, seg, seg - 1))), a


def _hgrn_prompt_kernel(q_ref, f_ref, i_ref, g_ref, lb_ref, ng_ref, w_ref, m_ref, o_ref, s_ref, st_ref,
                        *, hb, c, layer, eps):
    ci = pl.program_id(2)

    @pl.when(ci == 0)
    def _():
        st_ref[...] = jnp.zeros_like(st_ref)

    w = w_ref[...]
    for hh in range(hb):
        sl = slice(hh * LANES, (hh + 1) * LANES)
        lb = _hgrn_lower_bound(lb_ref[:, sl], layer)
        q, lg, k = _hgrn_gates(q_ref[:, sl], f_ref[:, sl], lb, LANES)
        xq, xk, a = _hgrn_intra(q, k, lg, w, m_ref, c)
        vb = i_ref[:, sl].astype(BF16)
        st = st_ref[hh]
        o = _dot_nt((q * xq).astype(BF16), st.astype(BF16)) + _dot(a.astype(BF16), vb)
        st_ref[hh] = st * xq[c - 1:c] + _dot_tn(vb, (k * xk).astype(BF16))
        o_ref[:, sl] = (_rms(o, ng_ref[...], eps) * _silu(g_ref[:, sl])).astype(o_ref.dtype)

    @pl.when(ci == pl.num_programs(2) - 1)
    def _():
        for hh in range(hb):
            s_ref[0, hh] = st_ref[hh].T


def _hgrn_prompt(proj, lb_logits, norm_g, n_seq, seq_len, *, layer, c, hb):
    hk = proj.shape[1] // 4
    nh = hk // LANES
    nc = seq_len // c
    ngrp = nh // hb
    w, masks = _hgrn_consts(c, c)
    bw = hb * LANES

    def col(part):
        return pl.BlockSpec((c, bw), lambda b, h, ci: (b * nc + ci, part * ngrp + h))

    return pl.pallas_call(
        functools.partial(_hgrn_prompt_kernel, hb=hb, c=c, layer=layer, eps=NORM_EPS),
        out_shape=(jax.ShapeDtypeStruct((n_seq * seq_len, hk), BF16),
                   jax.ShapeDtypeStruct((n_seq, nh, LANES, LANES), F32)),
        grid=(n_seq, ngrp, nc),
        in_specs=[
            col(0), col(1), col(2), col(3),
            pl.BlockSpec((lb_logits.shape[0], bw), lambda b, h, ci: (0, h)),
            pl.BlockSpec((1, LANES), lambda b, h, ci: (0, 0)),
            pl.BlockSpec(w.shape, lambda b, h, ci: (0, 0)),
            pl.BlockSpec(masks.shape, lambda b, h, ci: (0, 0, 0)),
        ],
        out_specs=(pl.BlockSpec((c, bw), lambda b, h, ci: (b * nc + ci, h)),
                   pl.BlockSpec((1, hb, LANES, LANES), lambda b, h, ci: (b, h, 0, 0))),
        scratch_shapes=[pltpu.VMEM((hb, LANES, LANES), F32)],
        compiler_params=_params(),
    )(proj, proj, proj, proj, lb_logits, norm_g.reshape(1, LANES), w, masks)


def _hgrn_sample_kernel(q_ref, f_ref, i_ref, g_ref, lb_ref, ng_ref, w_ref, m_ref, s0_ref, o_ref, s_ref,
                        *, hb, nb, tq, layer, eps):
    c = nb * tq
    w = w_ref[...]
    seq_of_row = lax.broadcasted_iota(jnp.int32, (c, 1), 0) // tq
    for hh in range(hb):
        sl = slice(hh * LANES, (hh + 1) * LANES)
        lb = _hgrn_lower_bound(lb_ref[:, sl], layer)
        q, lg, k = _hgrn_gates(q_ref[:, sl], f_ref[:, sl], lb, LANES)
        xq, xk, a = _hgrn_intra(q, k, lg, w, m_ref, tq)
        vb = i_ref[:, sl].astype(BF16)
        q0 = (q * xq).astype(BF16)
        kl = k * xk
        o = _dot(a.astype(BF16), vb)
        for b in range(nb):
            mine = seq_of_row == b
            st = s0_ref[b, hh].T
            o = o + jnp.where(mine, _dot_nt(q0, st.astype(BF16)), 0.0)
            r_last = (b + 1) * tq - 1
            klb = jnp.where(mine, kl, 0.0).astype(BF16)
            s_ref[b, hh] = (st * xq[r_last:r_last + 1] + _dot_tn(vb, klb)).T
        o_ref[:, sl] = (_rms(o, ng_ref[...], eps) * _silu(g_ref[:, sl])).astype(o_ref.dtype)


def _hgrn_sample(proj, row0, state, lb_logits, norm_g, tq, *, layer, nb, hb):
    n_seq, nh = state.shape[0], state.shape[1]
    hk = proj.shape[1] // 4
    ngrp = nh // hb
    c = nb * tq
    rb0 = row0 // c
    w, masks = _hgrn_consts(c, tq)
    bw = hb * LANES

    def col(part):
        return pl.BlockSpec((c, bw), lambda sb, h: (rb0 + sb, part * ngrp + h))

    return pl.pallas_call(
        functools.partial(_hgrn_sample_kernel, hb=hb, nb=nb, tq=tq, layer=layer, eps=NORM_EPS),
        out_shape=(jax.ShapeDtypeStruct((n_seq * tq, hk), BF16),
                   jax.ShapeDtypeStruct(state.shape, F32)),
        grid=(n_seq // nb, ngrp),
        in_specs=[
            col(0), col(1), col(2), col(3),
            pl.BlockSpec((lb_logits.shape[0], bw), lambda sb, h: (0, h)),
            pl.BlockSpec((1, LANES), lambda sb, h: (0, 0)),
            pl.BlockSpec(w.shape, lambda sb, h: (0, 0)),
            pl.BlockSpec(masks.shape, lambda sb, h: (0, 0, 0)),
            pl.BlockSpec((nb, hb, LANES, LANES), lambda sb, h: (sb, h, 0, 0)),
        ],
        out_specs=(pl.BlockSpec((c, bw), lambda sb, h: (sb, h)),
                   pl.BlockSpec((nb, hb, LANES, LANES), lambda sb, h: (sb, h, 0, 0))),
        compiler_params=_params(),
    )(proj, proj, proj, proj, lb_logits, norm_g.reshape(1, LANES), w, masks, state)


def _diff_lambda(lam_ref, lam_init):
    lp = lam_ref[...]
    s1 = jnp.sum(lp[0:1] * lp[1:2], axis=-1, keepdims=True)
    s2 = jnp.sum(lp[2:3] * lp[3:4], axis=-1, keepdims=True)
    return jnp.exp(s1) - jnp.exp(s2) + lam_init


def _lanes(x, width):
    return x if width == LANES else jnp.concatenate([x] * (width // LANES), axis=1)


def _attn_prompt_kernel(qi_ref, kj_ref, lam_ref, sg_ref, q_ref, k_ref, v_ref, o_ref, m_ref, l_ref, acc_ref,
                        *, blk, lam_init, eps):
    pair = pl.program_id(2)
    i = qi_ref[pair]
    j = kj_ref[pair]
    hv = 2 * LANES

    @pl.when(j == 0)
    def _():
        m_ref[...] = jnp.full_like(m_ref, NEG_BIG)
        l_ref[...] = jnp.zeros_like(l_ref)
        acc_ref[...] = jnp.zeros_like(acc_ref)

    def step(diagonal):
        vb = v_ref[...]
        for s in range(2):
            sl = slice(s * LANES, (s + 1) * LANES)
            sc = _dot_nt(q_ref[:, sl], k_ref[:, sl])
            if diagonal:
                row = lax.broadcasted_iota(jnp.int32, (blk, blk), 0)
                colv = lax.broadcasted_iota(jnp.int32, (blk, blk), 1)
                sc = jnp.where(colv <= row, sc, NEG_BIG)
            m_prev = m_ref[s]
            m_new = jnp.maximum(m_prev, jnp.max(sc, axis=-1, keepdims=True))
            alpha = jnp.exp(m_prev - m_new)
            p = jnp.exp(sc - _lanes(m_new, blk))
            l_ref[s] = alpha * l_ref[s] + jnp.sum(p, axis=-1, keepdims=True)
            acc_ref[s] = _lanes(alpha, hv) * acc_ref[s] + _dot(p.astype(BF16), vb)
            m_ref[s] = m_new

    @pl.when(j < i)
    def _():
        step(False)

    @pl.when(j == i)
    def _():
        step(True)
        lam = _diff_lambda(lam_ref, lam_init)
        o = acc_ref[0] / _lanes(l_ref[0], hv) - lam * (acc_ref[1] / _lanes(l_ref[1], hv))
        o_ref[...] = (_rms(o, sg_ref[...], eps) * (1.0 - lam_init)).astype(o_ref.dtype)


def _attn_prompt(q, k, v, lam_p, subln_g, n_seq, seq_len, *, lam_init, blk):
    hv = 2 * LANES
    nh = q.shape[1] // hv
    nb = seq_len // blk
    pairs = [(i, j) for i in range(nb) for j in range(i + 1)]
    qi = jnp.asarray([p[0] for p in pairs], jnp.int32)
    kj = jnp.asarray([p[1] for p in pairs], jnp.int32)
    return pl.pallas_call(
        functools.partial(_attn_prompt_kernel, blk=blk, lam_init=lam_init, eps=SUBLN_EPS),
        out_shape=jax.ShapeDtypeStruct((n_seq * seq_len, nh * hv), BF16),
        grid_spec=pltpu.PrefetchScalarGridSpec(
            num_scalar_prefetch=2,
            grid=(n_seq, nh, len(pairs)),
            in_specs=[
                pl.BlockSpec(lam_p.shape, lambda b, h, p, qi, kj: (0, 0)),
                pl.BlockSpec((1, hv), lambda b, h, p, qi, kj: (0, 0)),
                pl.BlockSpec((blk, hv), lambda b, h, p, qi, kj: (b * nb + qi[p], h)),
                pl.BlockSpec((blk, hv), lambda b, h, p, qi, kj: (b * nb + kj[p], h)),
                pl.BlockSpec((blk, hv), lambda b, h, p, qi, kj: (b * nb + kj[p], h)),
            ],
            out_specs=pl.BlockSpec((blk, hv), lambda b, h, p, qi, kj: (b * nb + qi[p], h)),
            scratch_shapes=[pltpu.VMEM((2, blk, LANES), F32), pltpu.VMEM((2, blk, LANES), F32),
                            pltpu.VMEM((2, blk, hv), F32)],
        ),
        compiler_params=_params(),
    )(qi, kj, lam_p, subln_g.reshape(1, hv), q, k, v)


def _attn_sample_kernel(pt_ref, lam_ref, sg_ref, q_ref, kn_ref, vn_ref, *rest, pages, tq, nh, lam_init, eps):
    k_refs = rest[:pages]
    v_refs = rest[pages:3 * pages]
    o_ref, m_ref, l_ref, acc_ref = rest[3 * pages:]
    g = pl.program_id(1)
    hv = 2 * LANES
    rows = 2 * tq
    n_sub = 2 * nh

    @pl.when(g == 0)
    def _():
        m_ref[...] = jnp.full_like(m_ref, NEG_BIG)
        l_ref[...] = jnp.zeros_like(l_ref)
        acc_ref[...] = jnp.zeros_like(acc_ref)

    first = lax.broadcasted_iota(jnp.int32, (rows, 1), 0) < tq

    def scores(k_sub):
        out = []
        for h in range(nh):
            qa = q_ref[h * rows:(h + 1) * rows, :]
            out.append(jnp.where(first, _dot_nt(qa, k_sub(2 * h)), _dot_nt(qa, k_sub(2 * h + 1))))
        return jnp.concatenate(out, axis=0)

    def update(sc, v_heads):
        nkeys = sc.shape[1] // len(v_heads)
        m_prev = m_ref[...]
        m_new = jnp.maximum(m_prev, jnp.max(sc, axis=-1, keepdims=True))
        alpha = jnp.exp(m_prev - m_new)
        p = jnp.exp(sc - m_new)
        l_ref[...] = alpha * l_ref[...] + jnp.sum(p, axis=-1, keepdims=True)
        pb = p.astype(BF16)
        pv = []
        for h in range(nh):
            parts = [_dot(pb[h * rows:(h + 1) * rows, b * nkeys:(b + 1) * nkeys], v_head(h))
                     for b, v_head in enumerate(v_heads)]
            pv.append(functools.reduce(lambda x, y: x + y, parts))
        acc_ref[...] = alpha * acc_ref[...] + jnp.concatenate(pv, axis=0)
        m_ref[...] = m_new

    psz = k_refs[0].shape[0] // n_sub

    def k_sub_of(r):
        return lambda j: r[pl.ds(j, psz, stride=n_sub), :].astype(BF16)

    def v_head_of(lo, hi):
        return lambda h: jnp.concatenate([lo[pl.ds(h, psz, stride=nh), :], hi[pl.ds(h, psz, stride=nh), :]],
                                         axis=1).astype(BF16)

    update(jnp.concatenate([scores(k_sub_of(k_refs[pg])) for pg in range(pages)], axis=1),
           [v_head_of(v_refs[2 * pg], v_refs[2 * pg + 1]) for pg in range(pages)])

    @pl.when(g == pl.num_programs(1) - 1)
    def _():
        pad = jnp.zeros((LANES - tq, kn_ref.shape[1]), F32)
        kb = jnp.concatenate([kn_ref[...], pad], axis=0).astype(BF16)
        vb = jnp.concatenate([vn_ref[...], pad], axis=0).astype(BF16)
        sc = scores(lambda j: kb[:, j * LANES:(j + 1) * LANES])
        qpos = lax.broadcasted_iota(jnp.int32, sc.shape, 0) % tq
        kpos = lax.broadcasted_iota(jnp.int32, sc.shape, 1)
        update(jnp.where(kpos <= qpos, sc, NEG_BIG), [lambda h: vb[:, h * hv:(h + 1) * hv]])
        lam = _diff_lambda(lam_ref, lam_init)
        attn = acc_ref[...] / l_ref[...]
        sg = sg_ref[...]
        for h in range(nh):
            o = attn[h * rows:h * rows + tq] - lam * attn[h * rows + tq:(h + 1) * rows]
            o_ref[:, h * hv:(h + 1) * hv] = _rms(o, sg, eps) * (1.0 - lam_init)


def _attn_sample(q16, k_new, v_new, cache_k, cache_v, page_table, lam_p, subln_g, tq, *, lam_init, pages):
    n_seq, n_pages = page_table.shape
    hv = 2 * LANES
    width = k_new.shape[1]
    nh = width // hv
    rows = nh * 2 * tq

    def page_spec(cache, pg, lane_blk):
        return pl.BlockSpec((None, cache.shape[1], LANES),
                            lambda b, g, pt: (pt[b * n_pages + g * pages + pg], 0, lane_blk))

    return pl.pallas_call(
        functools.partial(_attn_sample_kernel, pages=pages, tq=tq, nh=nh, lam_init=lam_init, eps=SUBLN_EPS),
        out_shape=jax.ShapeDtypeStruct((n_seq, tq, width), F32),
        grid_spec=pltpu.PrefetchScalarGridSpec(
            num_scalar_prefetch=1,
            grid=(n_seq, n_pages // pages),
            in_specs=[
                pl.BlockSpec(lam_p.shape, lambda b, g, pt: (0, 0)),
                pl.BlockSpec((1, hv), lambda b, g, pt: (0, 0)),
                pl.BlockSpec((None, rows, LANES), lambda b, g, pt: (b, 0, 0)),
                pl.BlockSpec((tq, width), lambda b, g, pt: (b, 0)),
                pl.BlockSpec((tq, width), lambda b, g, pt: (b, 0)),
            ] + [page_spec(cache_k, pg, 0) for pg in range(pages)]
              + [page_spec(cache_v, pg, half) for pg in range(pages) for half in range(2)],
            out_specs=pl.BlockSpec((None, tq, width), lambda b, g, pt: (b, 0, 0)),
            scratch_shapes=[pltpu.VMEM((rows, 1), F32), pltpu.VMEM((rows, 1), F32), pltpu.VMEM((rows, hv), F32)],
        ),
        compiler_params=_params(),
    )(page_table.reshape(-1), lam_p, subln_g.reshape(1, hv), q16, k_new, v_new,
      *([cache_k] * pages), *([cache_v] * (2 * pages)))


def _moe_route_kernel(y_ref, g_ref, wr_ref, hn_ref, r_ref, *, eps, n_exp):
    h = _rms(y_ref[...], g_ref[...], eps)
    hn_ref[...] = h
    h1, h2, h3 = _split3(h)
    w1, w2, w3 = _split3(wr_ref[...])
    logits = (_dot(h1, w1) + (_dot(h1, w2) + _dot(h2, w1)) + (_dot(h1, w3) + _dot(h2, w2) + _dot(h3, w1)))
    lane = lax.broadcasted_iota(jnp.int32, logits.shape, 1)
    logits = jnp.where(lane < n_exp, logits, NEG_BIG)
    m1 = jnp.max(logits, axis=-1, keepdims=True)
    i1 = jnp.min(jnp.where(logits == m1, lane, LANES), axis=-1, keepdims=True)
    rest = jnp.where(lane == i1, NEG_BIG, logits)
    m2 = jnp.max(rest, axis=-1, keepdims=True)
    i2 = jnp.min(jnp.where(rest == m2, lane, LANES), axis=-1, keepdims=True)
    e2 = jnp.exp(m2 - m1)
    w_top1 = 1.0 / (1.0 + e2)
    w_top2 = e2 / (1.0 + e2)
    r_ref[...] = jnp.where(lane == 0, i1.astype(F32),
                           jnp.where(lane == 1, i2.astype(F32),
                                     jnp.where(lane == 2, w_top1, jnp.where(lane == 3, w_top2, 0.0))))


def _moe_route(y, g, w_router, *, bm):
    t, d = y.shape
    n_exp = w_router.shape[1]
    wr = jnp.pad(w_router, ((0, 0), (0, LANES - n_exp)))
    return pl.pallas_call(
        functools.partial(_moe_route_kernel, eps=NORM_EPS, n_exp=n_exp),
        out_shape=(jax.ShapeDtypeStruct((t, d), F32), jax.ShapeDtypeStruct((t, LANES), F32)),
        grid=(t // bm,),
        in_specs=[
            pl.BlockSpec((bm, d), lambda i: (i, 0)),
            pl.BlockSpec((1, d), lambda i: (0, 0)),
            pl.BlockSpec((d, LANES), lambda i: (0, 0)),
        ],
        out_specs=(pl.BlockSpec((bm, d), lambda i: (i, 0)), pl.BlockSpec((bm, LANES), lambda i: (i, 0))),
        compiler_params=_params(),
    )(y, g.reshape(1, d), wr)


def _row_copy(src_hbm, dst, src_row, dst_row, sem):
    return pltpu.make_async_copy(src_hbm.at[pl.ds(src_row, 1)], dst.at[pl.ds(dst_row, 1)], sem)


def _moe_expert_kernel(te_ref, nr_ref, tok_ref, hn_ref, gate_ref, wg_ref, wu_ref, wd_ref, o_ref, xin_ref, xb_ref, sem,
                       *, bm, buckets):
    i = pl.program_id(0)
    f = pl.program_id(1)
    nrows = nr_ref[i]

    def groups(tile):
        return (nr_ref[tile] + SUBLANES - 1) // SUBLANES

    def issue(tile):
        def body(grp, carry):
            for u in range(SUBLANES):
                r = grp * SUBLANES + u
                _row_copy(hn_ref, xin_ref, tok_ref[tile * bm + r], r, sem).start()
            return carry
        lax.fori_loop(0, groups(tile), body, 0)

    @pl.when(f == 0)
    def _():
        @pl.when(i == 0)
        def _():
            xin_ref[...] = jnp.zeros_like(xin_ref)
            issue(0)

        def drain(grp, carry):
            for u in range(SUBLANES):
                _row_copy(hn_ref, xin_ref, 0, grp * SUBLANES + u, sem).wait()
            return carry
        lax.fori_loop(0, groups(i), drain, 0)
        row = lax.broadcasted_iota(jnp.int32, (bm, 1), 0)
        xb_ref[...] = jnp.where(row < nrows, xin_ref[...], 0.0).astype(BF16)
        o_ref[...] = jnp.zeros_like(o_ref)

        @pl.when(i + 1 < pl.num_programs(0))
        def _():
            issue(i + 1)

    for lo, m_rows in zip((0,) + tuple(buckets[:-1]), buckets):
        @pl.when((nrows > lo) & (nrows <= m_rows))
        def _(m_rows=m_rows):
            x = xb_ref[0:m_rows]
            gate = _dot(x, wg_ref[...].astype(BF16))
            up = _dot(x, wu_ref[...].astype(BF16))
            act = (_silu(gate) * up).astype(BF16)
            o_ref[0:m_rows] += _dot(act, wd_ref[...].astype(BF16))

    @pl.when(f == pl.num_programs(1) - 1)
    def _():
        o_ref[...] = o_ref[...] * gate_ref[...]


def _moe_experts(hn, token_sorted, gate_sorted, tile_expert, tile_rows, w_gu, w_down, *, bm, bf, buckets):
    d = hn.shape[1]
    n_tiles = tile_expert.shape[0]
    dff = w_down.shape[1]
    nf = dff // bf

    def fblk(i, f, nr):
        return jnp.where(nr[i] > 0, f, nf - 1)

    return pl.pallas_call(
        functools.partial(_moe_expert_kernel, bm=bm, buckets=buckets),
        out_shape=jax.ShapeDtypeStruct((n_tiles * bm, d), F32),
        grid_spec=pltpu.PrefetchScalarGridSpec(
            num_scalar_prefetch=3,
            grid=(n_tiles, nf),
            in_specs=[
                pl.BlockSpec(memory_space=pl.ANY),
                pl.BlockSpec((bm, 1), lambda i, f, te, nr, tok: (i, 0)),
                pl.BlockSpec((None, d, bf), lambda i, f, te, nr, tok: (te[i], 0, fblk(i, f, nr))),
                pl.BlockSpec((None, d, bf), lambda i, f, te, nr, tok: (te[i], 0, nf + fblk(i, f, nr))),
                pl.BlockSpec((None, bf, d), lambda i, f, te, nr, tok: (te[i], fblk(i, f, nr), 0)),
            ],
            out_specs=pl.BlockSpec((bm, d), lambda i, f, te, nr, tok: (i, 0), pipeline_mode=pl.Buffered(1)),
            scratch_shapes=[pltpu.VMEM((bm, d), F32), pltpu.VMEM((bm, d), BF16), pltpu.SemaphoreType.DMA],
        ),
        compiler_params=_params(),
    )(tile_expert, tile_rows, token_sorted, hn, gate_sorted.reshape(n_tiles * bm, 1), w_gu, w_gu, w_down)


def _moe_combine_kernel(pos_ref, y_ref, g_ref, ys_ref, o_ref, buf_ref, sem, *, bt, eps):
    def issue(r, carry):
        _row_copy(ys_ref, buf_ref.at[0], pos_ref[0, 0, 2 * r], r, sem).start()
        _row_copy(ys_ref, buf_ref.at[1], pos_ref[0, 0, 2 * r + 1], r, sem).start()
        return carry

    lax.fori_loop(0, bt, issue, 0, unroll=SUBLANES)

    def drain(r, carry):
        _row_copy(ys_ref, buf_ref.at[0], 0, r, sem).wait()
        _row_copy(ys_ref, buf_ref.at[1], 0, r, sem).wait()
        return carry

    lax.fori_loop(0, bt, drain, 0, unroll=SUBLANES)
    z = y_ref[...] + (buf_ref[0] + buf_ref[1])
    o_ref[...] = _rms(z, g_ref[...], eps)


def _moe_combine_norm(y, ys, pos, g, row0, nrows, *, bt):
    d = y.shape[1]
    steps = nrows // bt
    rb = row0 // bt
    return pl.pallas_call(
        functools.partial(_moe_combine_kernel, bt=bt, eps=NORM_EPS),
        out_shape=jax.ShapeDtypeStruct((nrows, d), F32),
        grid=(steps,),
        in_specs=[
            pl.BlockSpec((1, 1, 2 * bt), lambda i: (i, 0, 0), memory_space=pltpu.SMEM),
            pl.BlockSpec((bt, d), lambda i: (rb + i, 0)),
            pl.BlockSpec((1, d), lambda i: (0, 0)),
            pl.BlockSpec(memory_space=pl.ANY),
        ],
        out_specs=pl.BlockSpec((bt, d), lambda i: (i, 0)),
        scratch_shapes=[pltpu.VMEM((2, bt, d), F32), pltpu.SemaphoreType.DMA],
        compiler_params=_params(),
    )(pos[2 * row0:2 * (row0 + nrows)].reshape(steps, 1, 2 * bt), y, g.reshape(1, d), ys)


def _exclusive_cumsum_rows(x):
    n, k = x.shape
    nblk = n // LANES
    xb = x.reshape(nblk, LANES, k).astype(F32)
    strict = jnp.asarray(np.tril(np.ones((LANES, LANES), np.float32), -1))
    within = jnp.einsum("ij,bjk->bik", strict, xb, precision=lax.Precision.HIGHEST)
    totals = jnp.sum(xb, axis=1)
    before = jnp.cumsum(totals, axis=0) - totals
    return (within + before[:, None, :]).astype(jnp.int32).reshape(n, k)


def _moe_plan(route, n_exp, bm, n_tiles):
    t = route.shape[0]
    experts = route[:, 0:2].astype(jnp.int32).reshape(-1)
    gate_bits = lax.bitcast_convert_type(route[:, 2:4].reshape(-1), jnp.int32)
    onehot = (experts[:, None] == jnp.arange(n_exp, dtype=jnp.int32)[None, :]).astype(jnp.int32)
    rank = jnp.sum(_exclusive_cumsum_rows(onehot) * onehot, axis=1)
    counts = jnp.sum(onehot, axis=0)
    tiles_per = (counts + bm - 1) // bm
    rows_per = (((counts + jnp.maximum(tiles_per, 1) - 1) // jnp.maximum(tiles_per, 1)) + 7) // 8 * 8
    rows_per = jnp.maximum(rows_per, 8)
    tile_end = jnp.cumsum(tiles_per)
    tile_start = tile_end - tiles_per
    dest = (tile_start[experts] + rank // rows_per[experts]) * bm + rank % rows_per[experts]
    slot_vals = jnp.stack([jnp.arange(2 * t, dtype=jnp.int32) // 2, gate_bits], axis=1)
    sorted_vals = jnp.zeros((n_tiles * bm, 2), jnp.int32).at[dest].set(slot_vals)
    token_sorted = sorted_vals[:, 0]
    gate_sorted = lax.bitcast_convert_type(sorted_vals[:, 1], F32)
    n_used = tile_end[-1]
    tiles = jnp.arange(n_tiles, dtype=jnp.int32)
    tile_expert = jnp.sum((jnp.minimum(tiles, n_used - 1)[:, None] >= tile_end[None, :]).astype(jnp.int32), axis=1)
    local = tiles - tile_start[tile_expert]
    tile_rows = jnp.clip(counts[tile_expert] - local * rows_per[tile_expert], 0, rows_per[tile_expert])
    tile_rows = jnp.where(tiles < n_used, tile_rows, 0)
    return token_sorted, gate_sorted, tile_expert.astype(jnp.int32), tile_rows.astype(jnp.int32), dest


def _moe_block_norm(y, g_ffn, w_router, w_gu, w_down, g_out, splits, *, bm_route, bm, bf, buckets, bt):
    t = y.shape[0]
    n_exp = w_router.shape[1]
    hn, route = _moe_route(y, g_ffn, w_router, bm=bm_route)
    n_tiles = (2 * t) // bm + n_exp
    token_sorted, gate_sorted, tile_expert, tile_rows, dest = _moe_plan(route, n_exp, bm, n_tiles)
    ys = _moe_experts(hn, token_sorted, gate_sorted, tile_expert, tile_rows, w_gu, w_down, bm=bm, bf=bf,
                      buckets=buckets)
    return [_moe_combine_norm(y, ys, dest, g_out, row0, nrows, bt=bt) for row0, nrows in splits]


def _rope_tables(positions, half, reps):
    inv = ROPE_THETA ** (-np.arange(half, dtype=np.float64) / half)
    ang = np.asarray(positions, np.float64)[:, None] * inv[None, :]
    cos, sin = np.cos(ang), np.sin(ang)
    cos2 = np.tile(np.concatenate([cos, cos], axis=1), (reps, 1))
    sin2 = np.tile(np.concatenate([-sin, sin], axis=1), (reps, 1))
    return jnp.asarray(cos2, F32), jnp.asarray(sin2, F32)


def _step(x_prompt, x_sample, cache_k, cache_v, state_hgrn, page_table, norm_mix_g, norm_ffn_g, norm_out_g,
          hg_w_in, hg_lb_logits, hg_norm_g, hg_w_out, da_w_in, da_lambda, da_subln_g, da_w_out,
          ffn_w_gu, ffn_w_down, moe_w_router, moe_w_gu, moe_w_down, *, cfg):
    nb_p, seq, d = x_prompt.shape
    nb_s, tq, _ = x_sample.shape
    tp, ts = nb_p * seq, nb_s * tq
    t = tp + ts
    past = page_table.shape[1] * cache_k.shape[2]
    dh = cache_k.shape[-1]
    n_sub = cache_k.shape[-2]
    nh_attn = n_sub // 2
    bm, bn, bmp = cfg["bm"], cfg["bn"], cfg["bm_prompt"]
    y = jnp.concatenate([x_prompt.reshape(tp, d), x_sample.reshape(ts, d)], axis=0)

    hk = hg_w_in.shape[2] // 4
    (proj,) = _norm_proj(y, norm_mix_g[0], hg_w_in[0], 0, 4 * hk, 0, t, eps=NORM_EPS, bm=bm, bn=cfg["bn_wide"],
                         out_dtypes=(F32,))
    o_p, st_p = _hgrn_prompt(proj, hg_lb_logits, hg_norm_g[0], nb_p, seq, layer=0, c=cfg["hg_chunk"], hb=cfg["hg_hb"])
    o_s, st_s = _hgrn_sample(proj, tp, state_hgrn[0], hg_lb_logits, hg_norm_g[0], tq, layer=0,
                             nb=cfg["hg_nb"], hb=cfg["hg_hb"])
    y = _matmul_residual(jnp.concatenate([o_p, o_s], axis=0), hg_w_out[0], y, bm=bm, bn=bn)
    y = _ffn_dense(y, norm_ffn_g[0], ffn_w_gu[0], ffn_w_down[0], eps=NORM_EPS, bm=cfg["ffn_bm"], bf=cfg["ffn_bf"])

    lam_init = 0.8 - 0.6 * math.exp(-0.3 * 1)
    qk = n_sub * dh
    g1 = norm_mix_g[1]
    w_in = da_w_in[0]
    cos_p, sin_p = _rope_tables(np.arange(seq), dh // 2, 1)
    cos_s, sin_s = _rope_tables(past + np.arange(tq), dh // 2, nb_s)
    proj_p = functools.partial(_norm_proj, y, g1, w_in, row0=0, nrows=tp, eps=NORM_EPS, bm=bmp, bn=bn)
    proj_s = functools.partial(_norm_proj, y, g1, w_in, row0=tp, nrows=ts, eps=NORM_EPS, bm=ts, bn=bn)
    q_scale = dh ** -0.5
    (q16_p,) = proj_p(col0=0, ncols=qk, out_dtypes=(BF16,), rope=(cos_p, sin_p, q_scale))
    (q16_s,) = proj_s(col0=0, ncols=qk, out_dtypes=(BF16,), rope=(cos_s, sin_s, q_scale))
    k32_p, k16_p = proj_p(col0=qk, ncols=qk, out_dtypes=(F32, BF16), rope=(cos_p, sin_p, 1.0))
    (k32_s,) = proj_s(col0=qk, ncols=qk, out_dtypes=(F32,), rope=(cos_s, sin_s, 1.0))
    v32_p, v16_p = proj_p(col0=2 * qk, ncols=qk, out_dtypes=(F32, BF16))
    (v32_s,) = proj_s(col0=2 * qk, ncols=qk, out_dtypes=(F32,))
    a_p = _attn_prompt(q16_p, k16_p, v16_p, da_lambda[0], da_subln_g[0], nb_p, seq, lam_init=lam_init,
                       blk=cfg["attn_blk"])
    q_s = q16_s.reshape(nb_s, tq, nh_attn, 2, dh).transpose(0, 2, 3, 1, 4).reshape(nb_s, nh_attn * 2 * tq, dh)
    ck = cache_k[0].reshape(cache_k.shape[1], cache_k.shape[2] * n_sub, dh)
    cv = cache_v[0].reshape(cache_v.shape[1], cache_v.shape[2] * nh_attn, 2 * dh)
    a_s = _attn_sample(q_s, k32_s, v32_s, ck, cv, page_table, da_lambda[0], da_subln_g[0], tq,
                       lam_init=lam_init, pages=cfg["attn_pages"])
    attn = jnp.concatenate([a_p, a_s.reshape(ts, qk).astype(BF16)], axis=0)
    y = _matmul_residual(attn, da_w_out[0], y, bm=bm, bn=bn)
    y_p, y_s = _moe_block_norm(y, norm_ffn_g[1], moe_w_router[0], moe_w_gu[0], moe_w_down[0], norm_out_g,
                               [(0, tp), (tp, ts)], bm_route=cfg["route_bm"], bm=cfg["moe_bm"], bf=cfg["moe_bf"],
                               buckets=cfg["moe_buckets"], bt=cfg["combine_bt"])

    return (y_p.reshape(nb_p, seq, d), y_s.reshape(nb_s, tq, d),
            k32_p.reshape(1, nb_p, seq, n_sub, dh), v32_p.reshape(1, nb_p, seq, nh_attn, 2 * dh),
            k32_s.reshape(1, nb_s, tq, n_sub, dh), v32_s.reshape(1, nb_s, tq, nh_attn, 2 * dh),
            st_p[None], st_s[None])


_CFG = dict(bm=1056, bn=512, bn_wide=1024, bm_prompt=1024, hg_chunk=128, hg_hb=8, hg_nb=8, ffn_bm=704, ffn_bf=512,
            attn_blk=512, attn_pages=8, route_bm=264, moe_bm=1152, moe_bf=512,
            moe_buckets=(256, 512, 768, 1024, 1088, 1152), combine_bt=256)


def kernel(x_prompt, x_sample, cache_k, cache_v, state_hgrn, page_table, norm_mix_g, norm_ffn_g, norm_out_g,
           hg_w_in, hg_lb_logits, hg_norm_g, hg_w_out, da_w_in, da_lambda, da_subln_g, da_w_out,
           ffn_w_gu, ffn_w_down, moe_w_router, moe_w_gu, moe_w_down):
    return _step(x_prompt, x_sample, cache_k, cache_v, state_hgrn, page_table, norm_mix_g, norm_ffn_g, norm_out_g,
                 hg_w_in, hg_lb_logits, hg_norm_g, hg_w_out, da_w_in, da_lambda, da_subln_g, da_w_out,
                 ffn_w_gu, ffn_w_down, moe_w_router, moe_w_gu, moe_w_down, cfg=_CFG)
```

```python
import functools
import math

import numpy as np
import jax
import jax.numpy as jnp
from jax import lax
from jax.experimental import pallas as pl
from jax.experimental.pallas import tpu as pltpu

F32 = jnp.float32
BF16 = jnp.bfloat16

LANES = 128
SUBLANES = 8
NORM_EPS = 1e-6
SUBLN_EPS = 1e-5
ROPE_THETA = 10000.0
NEG_BIG = -1e30
VMEM_LIMIT = 56 * 1024 * 1024


def _params(**kw):
    return pltpu.CompilerParams(vmem_limit_bytes=VMEM_LIMIT, **kw)


def _dot(a, b):
    return jnp.dot(a, b, preferred_element_type=F32)


def _dot_nt(a, b):
    return lax.dot_general(a, b, (((1,), (1,)), ((), ())), preferred_element_type=F32)


def _dot_tn(a, b):
    return lax.dot_general(a, b, (((0,), (0,)), ((), ())), preferred_element_type=F32)


def _sigmoid(x):
    return 1.0 / (1.0 + jnp.exp(-x))


def _silu(x):
    return x * _sigmoid(x)


def _rms(x, g, eps):
    return x * lax.rsqrt(jnp.mean(x * x, axis=-1, keepdims=True) + eps) * g


def _split2(x):
    hi = x.astype(BF16)
    return hi, (x - hi.astype(F32)).astype(BF16)


def _split3(x):
    hi = x.astype(BF16)
    r1 = x - hi.astype(F32)
    mid = r1.astype(BF16)
    lo = (r1 - mid.astype(F32)).astype(BF16)
    return hi, mid, lo


def _norm_proj_kernel(x_ref, g_ref, w_ref, *rest, eps, rope_scale, n_out):
    o_refs, h_ref = rest[-1 - n_out:-1], rest[-1]

    @pl.when(pl.program_id(1) == 0)
    def _():
        h_ref[...] = _rms(x_ref[...], g_ref[...], eps).astype(BF16)

    acc = _dot(h_ref[...], w_ref[...].astype(BF16))
    if rope_scale is None:
        for o_ref in o_refs:
            o_ref[...] = acc.astype(o_ref.dtype)
    else:
        cos = rest[0][...]
        sin = rest[1][...]
        for c in range(acc.shape[1] // LANES):
            sl = slice(c * LANES, (c + 1) * LANES)
            xh = acc[:, sl]
            r = (xh * cos + pltpu.roll(xh, LANES // 2, 1) * sin) * rope_scale
            for o_ref in o_refs:
                o_ref[:, sl] = r.astype(o_ref.dtype)


def _norm_proj(x, g, w, col0, ncols, row0, nrows, *, eps, bm, bn, out_dtypes, rope=None):
    d = x.shape[1]
    cb, rb = col0 // bn, row0 // bm
    in_specs = [
        pl.BlockSpec((bm, d), lambda i, j: (rb + i, 0)),
        pl.BlockSpec((1, d), lambda i, j: (0, 0)),
        pl.BlockSpec((d, bn), lambda i, j: (0, cb + j)),
    ]
    args = [x, g.reshape(1, d), w]
    scale = None
    if rope is not None:
        cos, sin, scale = rope
        period = cos.shape[0] // bm
        in_specs += [pl.BlockSpec((bm, LANES), lambda i, j: (i % period, 0))] * 2
        args += [cos, sin]
    return pl.pallas_call(
        functools.partial(_norm_proj_kernel, eps=eps, rope_scale=scale, n_out=len(out_dtypes)),
        out_shape=tuple(jax.ShapeDtypeStruct((nrows, ncols), dt) for dt in out_dtypes),
        grid=(nrows // bm, ncols // bn),
        in_specs=in_specs,
        out_specs=tuple(pl.BlockSpec((bm, bn), lambda i, j: (i, j)) for _ in out_dtypes),
        scratch_shapes=[pltpu.VMEM((bm, d), BF16)],
        compiler_params=_params(),
    )(*args)


def _norm_qkv_kernel(x_ref, g_ref, w_ref, cos_ref, sin_ref, q16_ref, k32_ref, k16_ref, v32_ref, v16_ref, h_ref,
                     *, eps, q_scale, nq):
    j = pl.program_id(1)

    @pl.when(j == 0)
    def _():
        h_ref[...] = _rms(x_ref[...], g_ref[...], eps).astype(BF16)

    acc = _dot(h_ref[...], w_ref[...].astype(BF16))

    def rotated():
        cos = cos_ref[...]
        sin = sin_ref[...]
        for c in range(acc.shape[1] // LANES):
            sl = slice(c * LANES, (c + 1) * LANES)
            xh = acc[:, sl]
            yield sl, xh * cos + pltpu.roll(xh, LANES // 2, 1) * sin

    @pl.when(j < nq)
    def _():
        for sl, r in rotated():
            q16_ref[:, sl] = (r * q_scale).astype(BF16)

    @pl.when((j >= nq) & (j < 2 * nq))
    def _():
        for sl, r in rotated():
            k32_ref[:, sl] = r
            k16_ref[:, sl] = r.astype(BF16)

    @pl.when(j >= 2 * nq)
    def _():
        v32_ref[...] = acc
        v16_ref[...] = acc.astype(BF16)


def _norm_qkv(x, g, w, row0, nrows, cos, sin, q_scale, *, eps, bm, bn):
    d = x.shape[1]
    width = w.shape[1] // 3
    nq = width // bn
    rb = row0 // bm
    period = cos.shape[0] // bm

    def sect(s):
        return lambda i, j: (i, jnp.clip(j - s * nq, 0, nq - 1))

    def out(dt):
        return jax.ShapeDtypeStruct((nrows, width), dt)

    return pl.pallas_call(
        functools.partial(_norm_qkv_kernel, eps=eps, q_scale=q_scale, nq=nq),
        out_shape=(out(BF16), out(F32), out(BF16), out(F32), out(BF16)),
        grid=(nrows // bm, 3 * nq),
        in_specs=[
            pl.BlockSpec((bm, d), lambda i, j: (rb + i, 0)),
            pl.BlockSpec((1, d), lambda i, j: (0, 0)),
            pl.BlockSpec((d, bn), lambda i, j: (0, j)),
            pl.BlockSpec((bm, LANES), lambda i, j: (i % period, 0)),
            pl.BlockSpec((bm, LANES), lambda i, j: (i % period, 0)),
        ],
        out_specs=(pl.BlockSpec((bm, bn), sect(0)), pl.BlockSpec((bm, bn), sect(1)), pl.BlockSpec((bm, bn), sect(1)),
                   pl.BlockSpec((bm, bn), sect(2)), pl.BlockSpec((bm, bn), sect(2))),
        scratch_shapes=[pltpu.VMEM((bm, d), BF16)],
        compiler_params=_params(),
    )(x, g.reshape(1, d), w, cos, sin)


def _matmul_residual_kernel(a_ref, w_ref, r_ref, o_ref):
    o_ref[...] = r_ref[...] + _dot(a_ref[...], w_ref[...].astype(BF16))


def _matmul_residual(a, w, res, *, bm, bn):
    t, k = a.shape
    n = w.shape[1]
    return pl.pallas_call(
        _matmul_residual_kernel,
        out_shape=jax.ShapeDtypeStruct((t, n), F32),
        grid=(t // bm, n // bn),
        in_specs=[
            pl.BlockSpec((bm, k), lambda i, j: (i, 0)),
            pl.BlockSpec((k, bn), lambda i, j: (0, j)),
            pl.BlockSpec((bm, bn), lambda i, j: (i, j)),
        ],
        out_specs=pl.BlockSpec((bm, bn), lambda i, j: (i, j)),
        compiler_params=_params(),
    )(a, w, res)


def _ffn_kernel(x_ref, g_ref, wg_ref, wu_ref, wd_ref, o_ref, h_ref, *, eps):
    @pl.when(pl.program_id(1) == 0)
    def _():
        x = x_ref[...]
        h_ref[...] = _rms(x, g_ref[...], eps).astype(BF16)
        o_ref[...] = x

    h = h_ref[...]
    gate = _dot(h, wg_ref[...].astype(BF16))
    up = _dot(h, wu_ref[...].astype(BF16))
    act = (_silu(gate) * up).astype(BF16)
    o_ref[...] += _dot(act, wd_ref[...].astype(BF16))


def _ffn_dense(y, g, w_gu, w_down, *, eps, bm, bf):
    t, d = y.shape
    dff = w_down.shape[0]
    nf = dff // bf
    return pl.pallas_call(
        functools.partial(_ffn_kernel, eps=eps),
        out_shape=jax.ShapeDtypeStruct((t, d), F32),
        grid=(t // bm, nf),
        in_specs=[
            pl.BlockSpec((bm, d), lambda i, f: (i, 0)),
            pl.BlockSpec((1, d), lambda i, f: (0, 0)),
            pl.BlockSpec((d, bf), lambda i, f: (0, f)),
            pl.BlockSpec((d, bf), lambda i, f: (0, nf + f)),
            pl.BlockSpec((bf, d), lambda i, f: (f, 0)),
        ],
        out_specs=pl.BlockSpec((bm, d), lambda i, f: (i, 0)),
        scratch_shapes=[pltpu.VMEM((bm, d), BF16)],
        compiler_params=_params(),
    )(y, g.reshape(1, d), w_gu, w_gu, w_down)


def _hgrn_levels(seg):
    levels = []
    m = seg // 2
    while m >= 1:
        levels.append(m)
        m //= 2
    return levels


def _hgrn_consts(c, seg):
    t = np.arange(c)[:, None]
    p = np.arange(c)[None, :]
    same = (t // seg) == (p // seg)
    blocks = [(p <= t) & same]
    masks = []
    for m in _hgrn_levels(seg):
        r = (t // (2 * m)) * (2 * m) + m - 1
        upper = (t % (2 * m)) >= m
        if 2 * m < SUBLANES:
            blocks.append(np.where(upper, (p > r) & (p <= t), (p > t) & (p <= r)))
        masks.append(((t // (2 * m)) == (p // (2 * m))) & upper & ((p % (2 * m)) < m))
    masks.append(t == p)
    w = np.concatenate(blocks, axis=0).astype(np.float32)
    return jnp.asarray(w, BF16), jnp.asarray(np.stack(masks).astype(np.float32))


def _hgrn_lower_bound(logits, layer):
    e = jnp.exp(logits - jnp.max(logits, axis=0, keepdims=True))
    sm = e / jnp.sum(e, axis=0, keepdims=True)
    return jnp.sum(sm[: layer + 1], axis=0, keepdims=True)


def _hgrn_gates(qp, fp, lb, kdim):
    q = _silu(qp) * (kdim ** -0.5)
    f = lb + (1.0 - lb) * _sigmoid(fp)
    return q, jnp.log(f), 1.0 - f


def _block_row(g, blk, off):
    parts = [jnp.broadcast_to(g[b * blk + off:b * blk + off + 1], (blk, g.shape[1])) for b in range(g.shape[0] // blk)]
    return parts[0] if len(parts) == 1 else jnp.concatenate(parts, axis=0)


def _hgrn_intra(q, k, lg, w, m_ref, seg):
    c = q.shape[0]
    levels = _hgrn_levels(seg)
    hi, lo = _split2(lg)
    e = _dot(w, hi) + _dot(w, lo)
    g = e[0:c]
    a = m_ref[len(levels)] * _dot_nt(q.astype(BF16), k.astype(BF16))
    short = 0
    for li, m in enumerate(levels):
        if 2 * m >= SUBLANES:
            xm = jnp.exp(-jnp.abs(g - _block_row(g, 2 * m, m - 1)))
        else:
            short += 1
            xm = jnp.exp(e[short * c:(short + 1) * c])
        a = a + m_ref[li] * _dot_nt((q * xm).astype(BF16), (k * xm).astype(BF16))
    return jnp.exp(g), jnp.exp(-jnp.abs(g - _block_row(g, seg, seg - 1))), a


def _hgrn_prompt_kernel(q_ref, f_ref, i_ref, g_ref, lb_ref, ng_ref, w_ref, m_ref, o_ref, s_ref, st_ref,
                        *, hb, c, layer, eps):
    ci = pl.program_id(2)

    @pl.when(ci == 0)
    def _():
        st_ref[...] = jnp.zeros_like(st_ref)

    w = w_ref[...]
    for hh in range(hb):
        sl = slice(hh * LANES, (hh + 1) * LANES)
        lb = _hgrn_lower_bound(lb_ref[:, sl], layer)
        q, lg, k = _hgrn_gates(q_ref[:, sl], f_ref[:, sl], lb, LANES)
        xq, xk, a = _hgrn_intra(q, k, lg, w, m_ref, c)
        vb = i_ref[:, sl].astype(BF16)
        st = st_ref[hh]
        o = _dot_nt((q * xq).astype(BF16), st.astype(BF16)) + _dot(a.astype(BF16), vb)
        st_ref[hh] = st * xq[c - 1:c] + _dot_tn(vb, (k * xk).astype(BF16))
        o_ref[:, sl] = (_rms(o, ng_ref[...], eps) * _silu(g_ref[:, sl])).astype(o_ref.dtype)

    @pl.when(ci == pl.num_programs(2) - 1)
    def _():
        for hh in range(hb):
            s_ref[0, hh] = st_ref[hh].T


def _hgrn_prompt(proj, lb_logits, norm_g, n_seq, seq_len, *, layer, c, hb):
    hk = proj.shape[1] // 4
    nh = hk // LANES
    nc = seq_len // c
    ngrp = nh // hb
    w, masks = _hgrn_consts(c, c)
    bw = hb * LANES

    def col(part):
        return pl.BlockSpec((c, bw), lambda b, h, ci: (b * nc + ci, part * ngrp + h))

    return pl.pallas_call(
        functools.partial(_hgrn_prompt_kernel, hb=hb, c=c, layer=layer, eps=NORM_EPS),
        out_shape=(jax.ShapeDtypeStruct((n_seq * seq_len, hk), BF16),
                   jax.ShapeDtypeStruct((n_seq, nh, LANES, LANES), F32)),
        grid=(n_seq, ngrp, nc),
        in_specs=[
            col(0), col(1), col(2), col(3),
            pl.BlockSpec((lb_logits.shape[0], bw), lambda b, h, ci: (0, h)),
            pl.BlockSpec((1, LANES), lambda b, h, ci: (0, 0)),
            pl.BlockSpec(w.shape, lambda b, h, ci: (0, 0)),
            pl.BlockSpec(masks.shape, lambda b, h, ci: (0, 0, 0)),
        ],
        out_specs=(pl.BlockSpec((c, bw), lambda b, h, ci: (b * nc + ci, h)),
                   pl.BlockSpec((1, hb, LANES, LANES), lambda b, h, ci: (b, h, 0, 0))),
        scratch_shapes=[pltpu.VMEM((hb, LANES, LANES), F32)],
        compiler_params=_params(),
    )(proj, proj, proj, proj, lb_logits, norm_g.reshape(1, LANES), w, masks)


def _hgrn_sample_kernel(q_ref, f_ref, i_ref, g_ref, lb_ref, ng_ref, w_ref, m_ref, s0_ref, o_ref, s_ref,
                        *, hb, nb, tq, layer, eps):
    c = nb * tq
    w = w_ref[...]
    seq_of_row = lax.broadcasted_iota(jnp.int32, (c, 1), 0) // tq
    for hh in range(hb):
        sl = slice(hh * LANES, (hh + 1) * LANES)
        lb = _hgrn_lower_bound(lb_ref[:, sl], layer)
        q, lg, k = _hgrn_gates(q_ref[:, sl], f_ref[:, sl], lb, LANES)
        xq, xk, a = _hgrn_intra(q, k, lg, w, m_ref, tq)
        vb = i_ref[:, sl].astype(BF16)
        q0 = (q * xq).astype(BF16)
        kl = k * xk
        o = _dot(a.astype(BF16), vb)
        for b in range(nb):
            mine = seq_of_row == b
            st = s0_ref[b, hh].T
            o = o + jnp.where(mine, _dot_nt(q0, st.astype(BF16)), 0.0)
            r_last = (b + 1) * tq - 1
            klb = jnp.where(mine, kl, 0.0).astype(BF16)
            s_ref[b, hh] = (st * xq[r_last:r_last + 1] + _dot_tn(vb, klb)).T
        o_ref[:, sl] = (_rms(o, ng_ref[...], eps) * _silu(g_ref[:, sl])).astype(o_ref.dtype)


def _hgrn_sample(proj, row0, state, lb_logits, norm_g, tq, *, layer, nb, hb):
    n_seq, nh = state.shape[0], state.shape[1]
    hk = proj.shape[1] // 4
    ngrp = nh // hb
    c = nb * tq
    rb0 = row0 // c
    w, masks = _hgrn_consts(c, tq)
    bw = hb * LANES

    def col(part):
        return pl.BlockSpec((c, bw), lambda sb, h: (rb0 + sb, part * ngrp + h))

    return pl.pallas_call(
        functools.partial(_hgrn_sample_kernel, hb=hb, nb=nb, tq=tq, layer=layer, eps=NORM_EPS),
        out_shape=(jax.ShapeDtypeStruct((n_seq * tq, hk), BF16),
                   jax.ShapeDtypeStruct(state.shape, F32)),
        grid=(n_seq // nb, ngrp),
        in_specs=[
            col(0), col(1), col(2), col(3),
            pl.BlockSpec((lb_logits.shape[0], bw), lambda sb, h: (0, h)),
            pl.BlockSpec((1, LANES), lambda sb, h: (0, 0)),
            pl.BlockSpec(w.shape, lambda sb, h: (0, 0)),
            pl.BlockSpec(masks.shape, lambda sb, h: (0, 0, 0)),
            pl.BlockSpec((nb, hb, LANES, LANES), lambda sb, h: (sb, h, 0, 0)),
        ],
        out_specs=(pl.BlockSpec((c, bw), lambda sb, h: (sb, h)),
                   pl.BlockSpec((nb, hb, LANES, LANES), lambda sb, h: (sb, h, 0, 0))),
        compiler_params=_params(),
    )(proj, proj, proj, proj, lb_logits, norm_g.reshape(1, LANES), w, masks, state)


def _diff_lambda(lam_ref, lam_init):
    lp = lam_ref[...]
    s1 = jnp.sum(lp[0:1] * lp[1:2], axis=-1, keepdims=True)
    s2 = jnp.sum(lp[2:3] * lp[3:4], axis=-1, keepdims=True)
    return jnp.exp(s1) - jnp.exp(s2) + lam_init


def _lanes(x, width):
    return x if width == LANES else jnp.concatenate([x] * (width // LANES), axis=1)


def _attn_prompt_kernel(qi_ref, kj_ref, lam_ref, sg_ref, q_ref, k_ref, v_ref, o_ref, m_ref, l_ref, acc_ref,
                        *, blk, lam_init, eps):
    pair = pl.program_id(2)
    i = qi_ref[pair]
    j = kj_ref[pair]
    hv = 2 * LANES

    @pl.when(j == 0)
    def _():
        m_ref[...] = jnp.full_like(m_ref, NEG_BIG)
        l_ref[...] = jnp.zeros_like(l_ref)
        acc_ref[...] = jnp.zeros_like(acc_ref)

    def step(diagonal):
        vb = v_ref[...]
        for s in range(2):
            sl = slice(s * LANES, (s + 1) * LANES)
            sc = _dot_nt(q_ref[:, sl], k_ref[:, sl])
            if diagonal:
                row = lax.broadcasted_iota(jnp.int32, (blk, blk), 0)
                colv = lax.broadcasted_iota(jnp.int32, (blk, blk), 1)
                sc = jnp.where(colv <= row, sc, NEG_BIG)
            m_prev = m_ref[s]
            m_new = jnp.maximum(m_prev, jnp.max(sc, axis=-1, keepdims=True))
            alpha = jnp.exp(m_prev - m_new)
            p = jnp.exp(sc - _lanes(m_new, blk))
            l_ref[s] = alpha * l_ref[s] + jnp.sum(p, axis=-1, keepdims=True)
            acc_ref[s] = _lanes(alpha, hv) * acc_ref[s] + _dot(p.astype(BF16), vb)
            m_ref[s] = m_new

    @pl.when(j < i)
    def _():
        step(False)

    @pl.when(j == i)
    def _():
        step(True)
        lam = _diff_lambda(lam_ref, lam_init)
        o = acc_ref[0] / _lanes(l_ref[0], hv) - lam * (acc_ref[1] / _lanes(l_ref[1], hv))
        o_ref[...] = (_rms(o, sg_ref[...], eps) * (1.0 - lam_init)).astype(o_ref.dtype)


def _attn_prompt(q, k, v, lam_p, subln_g, n_seq, seq_len, *, lam_init, blk):
    hv = 2 * LANES
    nh = q.shape[1] // hv
    nb = seq_len // blk
    pairs = [(i, j) for i in range(nb) for j in range(i + 1)]
    qi = jnp.asarray([p[0] for p in pairs], jnp.int32)
    kj = jnp.asarray([p[1] for p in pairs], jnp.int32)
    return pl.pallas_call(
        functools.partial(_attn_prompt_kernel, blk=blk, lam_init=lam_init, eps=SUBLN_EPS),
        out_shape=jax.ShapeDtypeStruct((n_seq * seq_len, nh * hv), BF16),
        grid_spec=pltpu.PrefetchScalarGridSpec(
            num_scalar_prefetch=2,
            grid=(n_seq, nh, len(pairs)),
            in_specs=[
                pl.BlockSpec(lam_p.shape, lambda b, h, p, qi, kj: (0, 0)),
                pl.BlockSpec((1, hv), lambda b, h, p, qi, kj: (0, 0)),
                pl.BlockSpec((blk, hv), lambda b, h, p, qi, kj: (b * nb + qi[p], h)),
                pl.BlockSpec((blk, hv), lambda b, h, p, qi, kj: (b * nb + kj[p], h)),
                pl.BlockSpec((blk, hv), lambda b, h, p, qi, kj: (b * nb + kj[p], h)),
            ],
            out_specs=pl.BlockSpec((blk, hv), lambda b, h, p, qi, kj: (b * nb + qi[p], h)),
            scratch_shapes=[pltpu.VMEM((2, blk, LANES), F32), pltpu.VMEM((2, blk, LANES), F32),
                            pltpu.VMEM((2, blk, hv), F32)],
        ),
        compiler_params=_params(),
    )(qi, kj, lam_p, subln_g.reshape(1, hv), q, k, v)


def _attn_sample_kernel(pt_ref, lam_ref, sg_ref, q_ref, kn_ref, vn_ref, *rest, pages, tq, nh, lam_init, eps):
    k_refs = rest[:pages]
    v_refs = rest[pages:3 * pages]
    o_ref, m_ref, l_ref, acc_ref = rest[3 * pages:]
    g = pl.program_id(1)
    hv = 2 * LANES
    rows = 2 * tq
    n_sub = 2 * nh

    @pl.when(g == 0)
    def _():
        m_ref[...] = jnp.full_like(m_ref, NEG_BIG)
        l_ref[...] = jnp.zeros_like(l_ref)
        acc_ref[...] = jnp.zeros_like(acc_ref)

    first = lax.broadcasted_iota(jnp.int32, (rows, 1), 0) < tq

    def scores(k_sub):
        out = []
        for h in range(nh):
            qa = q_ref[h * rows:(h + 1) * rows, :]
            out.append(jnp.where(first, _dot_nt(qa, k_sub(2 * h)), _dot_nt(qa, k_sub(2 * h + 1))))
        return jnp.concatenate(out, axis=0)

    def update(sc, v_heads):
        nkeys = sc.shape[1] // len(v_heads)
        m_prev = m_ref[...]
        m_new = jnp.maximum(m_prev, jnp.max(sc, axis=-1, keepdims=True))
        alpha = jnp.exp(m_prev - m_new)
        p = jnp.exp(sc - m_new)
        l_ref[...] = alpha * l_ref[...] + jnp.sum(p, axis=-1, keepdims=True)
        pb = p.astype(BF16)
        pv = []
        for h in range(nh):
            parts = [_dot(pb[h * rows:(h + 1) * rows, b * nkeys:(b + 1) * nkeys], v_head(h))
                     for b, v_head in enumerate(v_heads)]
            pv.append(functools.reduce(lambda x, y: x + y, parts))
        acc_ref[...] = alpha * acc_ref[...] + jnp.concatenate(pv, axis=0)
        m_ref[...] = m_new

    psz = k_refs[0].shape[0] // n_sub

    def k_sub_of(r):
        return lambda j: r[pl.ds(j, psz, stride=n_sub), :].astype(BF16)

    def v_head_of(lo, hi):
        return lambda h: jnp.concatenate([lo[pl.ds(h, psz, stride=nh), :], hi[pl.ds(h, psz, stride=nh), :]],
                                         axis=1).astype(BF16)

    update(jnp.concatenate([scores(k_sub_of(k_refs[pg])) for pg in range(pages)], axis=1),
           [v_head_of(v_refs[2 * pg], v_refs[2 * pg + 1]) for pg in range(pages)])

    @pl.when(g == pl.num_programs(1) - 1)
    def _():
        pad = jnp.zeros((LANES - tq, kn_ref.shape[1]), F32)
        kb = jnp.concatenate([kn_ref[...], pad], axis=0).astype(BF16)
        vb = jnp.concatenate([vn_ref[...], pad], axis=0).astype(BF16)
        sc = scores(lambda j: kb[:, j * LANES:(j + 1) * LANES])
        qpos = lax.broadcasted_iota(jnp.int32, sc.shape, 0) % tq
        kpos = lax.broadcasted_iota(jnp.int32, sc.shape, 1)
        update(jnp.where(kpos <= qpos, sc, NEG_BIG), [lambda h: vb[:, h * hv:(h + 1) * hv]])
        lam = _diff_lambda(lam_ref, lam_init)
        attn = acc_ref[...] / l_ref[...]
        sg = sg_ref[...]
        for h in range(nh):
            o = attn[h * rows:h * rows + tq] - lam * attn[h * rows + tq:(h + 1) * rows]
            o_ref[:, h * hv:(h + 1) * hv] = _rms(o, sg, eps) * (1.0 - lam_init)


def _attn_sample(q16, k_new, v_new, cache_k, cache_v, page_table, lam_p, subln_g, tq, *, lam_init, pages):
    n_seq, n_pages = page_table.shape
    hv = 2 * LANES
    width = k_new.shape[1]
    nh = width // hv
    rows = nh * 2 * tq

    def page_spec(cache, pg, lane_blk):
        return pl.BlockSpec((None, cache.shape[1], LANES),
                            lambda b, g, pt: (pt[b * n_pages + g * pages + pg], 0, lane_blk))

    return pl.pallas_call(
        functools.partial(_attn_sample_kernel, pages=pages, tq=tq, nh=nh, lam_init=lam_init, eps=SUBLN_EPS),
        out_shape=jax.ShapeDtypeStruct((n_seq, tq, width), F32),
        grid_spec=pltpu.PrefetchScalarGridSpec(
            num_scalar_prefetch=1,
            grid=(n_seq, n_pages // pages),
            in_specs=[
                pl.BlockSpec(lam_p.shape, lambda b, g, pt: (0, 0)),
                pl.BlockSpec((1, hv), lambda b, g, pt: (0, 0)),
                pl.BlockSpec((None, rows, LANES), lambda b, g, pt: (b, 0, 0)),
                pl.BlockSpec((tq, width), lambda b, g, pt: (b, 0)),
                pl.BlockSpec((tq, width), lambda b, g, pt: (b, 0)),
            ] + [page_spec(cache_k, pg, 0) for pg in range(pages)]
              + [page_spec(cache_v, pg, half) for pg in range(pages) for half in range(2)],
            out_specs=pl.BlockSpec((None, tq, width), lambda b, g, pt: (b, 0, 0)),
            scratch_shapes=[pltpu.VMEM((rows, 1), F32), pltpu.VMEM((rows, 1), F32), pltpu.VMEM((rows, hv), F32)],
        ),
        compiler_params=_params(),
    )(page_table.reshape(-1), lam_p, subln_g.reshape(1, hv), q16, k_new, v_new,
      *([cache_k] * pages), *([cache_v] * (2 * pages)))


def _moe_route_kernel(y_ref, g_ref, wr_ref, hn_ref, r_ref, *, eps, n_exp):
    h = _rms(y_ref[...], g_ref[...], eps)
    hn_ref[...] = h
    h1, h2, h3 = _split3(h)
    w1, w2, w3 = _split3(wr_ref[...])
    logits = (_dot(h1, w1) + (_dot(h1, w2) + _dot(h2, w1)) + (_dot(h1, w3) + _dot(h2, w2) + _dot(h3, w1)))
    lane = lax.broadcasted_iota(jnp.int32, logits.shape, 1)
    logits = jnp.where(lane < n_exp, logits, NEG_BIG)
    m1 = jnp.max(logits, axis=-1, keepdims=True)
    i1 = jnp.min(jnp.where(logits == m1, lane, LANES), axis=-1, keepdims=True)
    rest = jnp.where(lane == i1, NEG_BIG, logits)
    m2 = jnp.max(rest, axis=-1, keepdims=True)
    i2 = jnp.min(jnp.where(rest == m2, lane, LANES), axis=-1, keepdims=True)
    e2 = jnp.exp(m2 - m1)
    w_top1 = 1.0 / (1.0 + e2)
    w_top2 = e2 / (1.0 + e2)
    r_ref[...] = jnp.where(lane == 0, i1.astype(F32),
                           jnp.where(lane == 1, i2.astype(F32),
                                     jnp.where(lane == 2, w_top1, jnp.where(lane == 3, w_top2, 0.0))))


def _moe_route(y, g, w_router, *, bm):
    t, d = y.shape
    n_exp = w_router.shape[1]
    wr = jnp.pad(w_router, ((0, 0), (0, LANES - n_exp)))
    return pl.pallas_call(
        functools.partial(_moe_route_kernel, eps=NORM_EPS, n_exp=n_exp),
        out_shape=(jax.ShapeDtypeStruct((t, d), F32), jax.ShapeDtypeStruct((t, LANES), F32)),
        grid=(t // bm,),
        in_specs=[
            pl.BlockSpec((bm, d), lambda i: (i, 0)),
            pl.BlockSpec((1, d), lambda i: (0, 0)),
            pl.BlockSpec((d, LANES), lambda i: (0, 0)),
        ],
        out_specs=(pl.BlockSpec((bm, d), lambda i: (i, 0)), pl.BlockSpec((bm, LANES), lambda i: (i, 0))),
        compiler_params=_params(),
    )(y, g.reshape(1, d), wr)


def _row_copy(src_hbm, dst, src_row, dst_row, sem):
    return pltpu.make_async_copy(src_hbm.at[pl.ds(src_row, 1)], dst.at[pl.ds(dst_row, 1)], sem)


def _moe_expert_kernel(te_ref, nr_ref, tok_ref, hn_ref, gate_ref, wg_ref, wu_ref, wd_ref, o_ref, xin_ref, xb_ref, sem,
                       *, bm, buckets):
    i = pl.program_id(0)
    f = pl.program_id(1)
    nrows = nr_ref[i]

    def groups(tile):
        return (nr_ref[tile] + SUBLANES - 1) // SUBLANES

    def issue(tile):
        def body(grp, carry):
            for u in range(SUBLANES):
                r = grp * SUBLANES + u
                _row_copy(hn_ref, xin_ref, tok_ref[tile * bm + r], r, sem).start()
            return carry
        lax.fori_loop(0, groups(tile), body, 0)

    @pl.when(f == 0)
    def _():
        @pl.when(i == 0)
        def _():
            xin_ref[...] = jnp.zeros_like(xin_ref)
            issue(0)

        def drain_rows(n):
            def body(c, carry):
                pltpu.make_async_copy(hn_ref.at[pl.ds(0, n)], xin_ref.at[pl.ds(0, n)], sem).wait()
                return carry
            return body
        ngrp = groups(i)
        lax.fori_loop(0, ngrp // SUBLANES, drain_rows(SUBLANES * SUBLANES), 0)
        lax.fori_loop(0, ngrp % SUBLANES, drain_rows(SUBLANES), 0)
        row = lax.broadcasted_iota(jnp.int32, (bm, 1), 0)
        xb_ref[...] = jnp.where(row < nrows, xin_ref[...], 0.0).astype(BF16)
        o_ref[...] = jnp.zeros_like(o_ref)

        @pl.when(i + 1 < pl.num_programs(0))
        def _():
            issue(i + 1)

    for lo, m_rows in zip((0,) + tuple(buckets[:-1]), buckets):
        @pl.when((nrows > lo) & (nrows <= m_rows))
        def _(m_rows=m_rows):
            x = xb_ref[0:m_rows]
            gate = _dot(x, wg_ref[...].astype(BF16))
            up = _dot(x, wu_ref[...].astype(BF16))
            act = (_silu(gate) * up).astype(BF16)
            o_ref[0:m_rows] += _dot(act, wd_ref[...].astype(BF16))

    @pl.when(f == pl.num_programs(1) - 1)
    def _():
        o_ref[...] = o_ref[...] * gate_ref[...]


def _moe_experts(hn, token_sorted, gate_sorted, tile_expert, tile_rows, w_gu, w_down, *, bm, bf, buckets):
    d = hn.shape[1]
    n_tiles = tile_expert.shape[0]
    dff = w_down.shape[1]
    nf = dff // bf

    def fblk(i, f, nr):
        return jnp.where(nr[i] > 0, f, nf - 1)

    return pl.pallas_call(
        functools.partial(_moe_expert_kernel, bm=bm, buckets=buckets),
        out_shape=jax.ShapeDtypeStruct((n_tiles * bm, d), F32),
        grid_spec=pltpu.PrefetchScalarGridSpec(
            num_scalar_prefetch=3,
            grid=(n_tiles, nf),
            in_specs=[
                pl.BlockSpec(memory_space=pl.ANY),
                pl.BlockSpec((bm, 1), lambda i, f, te, nr, tok: (i, 0)),
                pl.BlockSpec((None, d, bf), lambda i, f, te, nr, tok: (te[i], 0, fblk(i, f, nr))),
                pl.BlockSpec((None, d, bf), lambda i, f, te, nr, tok: (te[i], 0, nf + fblk(i, f, nr))),
                pl.BlockSpec((None, bf, d), lambda i, f, te, nr, tok: (te[i], fblk(i, f, nr), 0)),
            ],
            out_specs=pl.BlockSpec((bm, d), lambda i, f, te, nr, tok: (i, 0), pipeline_mode=pl.Buffered(1)),
            scratch_shapes=[pltpu.VMEM((bm, d), F32), pltpu.VMEM((bm, d), BF16), pltpu.SemaphoreType.DMA],
        ),
        compiler_params=_params(),
    )(tile_expert, tile_rows, token_sorted, hn, gate_sorted.reshape(n_tiles * bm, 1), w_gu, w_gu, w_down)


def _moe_combine_kernel(pos_ref, y_ref, g_ref, ys_ref, o_ref, buf_ref, sem, *, bt, eps):
    def issue(r, carry):
        _row_copy(ys_ref, buf_ref.at[0], pos_ref[0, 0, 2 * r], r, sem).start()
        _row_copy(ys_ref, buf_ref.at[1], pos_ref[0, 0, 2 * r + 1], r, sem).start()
        return carry

    lax.fori_loop(0, bt, issue, 0, unroll=SUBLANES)

    for half in range(2):
        pltpu.make_async_copy(ys_ref.at[pl.ds(0, bt)], buf_ref.at[half], sem).wait()
    z = y_ref[...] + (buf_ref[0] + buf_ref[1])
    o_ref[...] = _rms(z, g_ref[...], eps)


def _moe_combine_norm(y, ys, pos, g, row0, nrows, *, bt):
    d = y.shape[1]
    steps = nrows // bt
    rb = row0 // bt
    return pl.pallas_call(
        functools.partial(_moe_combine_kernel, bt=bt, eps=NORM_EPS),
        out_shape=jax.ShapeDtypeStruct((nrows, d), F32),
        grid=(steps,),
        in_specs=[
            pl.BlockSpec((1, 1, 2 * bt), lambda i: (i, 0, 0), memory_space=pltpu.SMEM),
            pl.BlockSpec((bt, d), lambda i: (rb + i, 0)),
            pl.BlockSpec((1, d), lambda i: (0, 0)),
            pl.BlockSpec(memory_space=pl.ANY),
        ],
        out_specs=pl.BlockSpec((bt, d), lambda i: (i, 0)),
        scratch_shapes=[pltpu.VMEM((2, bt, d), F32), pltpu.SemaphoreType.DMA],
        compiler_params=_params(),
    )(pos[2 * row0:2 * (row0 + nrows)].reshape(steps, 1, 2 * bt), y, g.reshape(1, d), ys)


def _exclusive_cumsum_rows(x):
    n, k = x.shape
    nblk = n // LANES
    xb = x.reshape(nblk, LANES, k).astype(F32)
    strict = jnp.asarray(np.tril(np.ones((LANES, LANES), np.float32), -1))
    within = jnp.einsum("ij,bjk->bik", strict, xb, precision=lax.Precision.HIGHEST)
    totals = jnp.sum(xb, axis=1)
    before = jnp.cumsum(totals, axis=0) - totals
    return (within + before[:, None, :]).astype(jnp.int32).reshape(n, k)


def _moe_plan(route, n_exp, bm, n_tiles):
    t = route.shape[0]
    experts = route[:, 0:2].astype(jnp.int32).reshape(-1)
    gates = route[:, 2:4].reshape(-1)
    onehot = (experts[:, None] == jnp.arange(n_exp, dtype=jnp.int32)[None, :]).astype(jnp.int32)
    rank = jnp.sum(_exclusive_cumsum_rows(onehot) * onehot, axis=1)
    counts = jnp.sum(onehot, axis=0)
    tiles_per = (counts + bm - 1) // bm
    rows_per = (((counts + jnp.maximum(tiles_per, 1) - 1) // jnp.maximum(tiles_per, 1)) + 7) // 8 * 8
    rows_per = jnp.maximum(rows_per, 8)
    tile_end = jnp.cumsum(tiles_per)
    tile_start = tile_end - tiles_per
    dest = (tile_start[experts] + rank // rows_per[experts]) * bm + rank % rows_per[experts]
    slot_vals = jnp.stack([(jnp.arange(2 * t, dtype=jnp.int32) // 2).astype(F32), gates], axis=1)
    sorted_vals = jnp.zeros((n_tiles * bm, 2), F32).at[dest].set(slot_vals)
    token_sorted = sorted_vals[:, 0].astype(jnp.int32)
    gate_sorted = sorted_vals[:, 1]
    n_used = tile_end[-1]
    tiles = jnp.arange(n_tiles, dtype=jnp.int32)
    tile_expert = jnp.sum((jnp.minimum(tiles, n_used - 1)[:, None] >= tile_end[None, :]).astype(jnp.int32), axis=1)
    local = tiles - tile_start[tile_expert]
    tile_rows = jnp.clip(counts[tile_expert] - local * rows_per[tile_expert], 0, rows_per[tile_expert])
    tile_rows = jnp.where(tiles < n_used, tile_rows, 0)
    return token_sorted, gate_sorted, tile_expert.astype(jnp.int32), tile_rows.astype(jnp.int32), dest


def _moe_block_norm(y, g_ffn, w_router, w_gu, w_down, g_out, splits, *, bm_route, bm, bf, buckets, bt):
    t = y.shape[0]
    n_exp = w_router.shape[1]
    hn, route = _moe_route(y, g_ffn, w_router, bm=bm_route)
    n_tiles = (2 * t) // bm + n_exp
    token_sorted, gate_sorted, tile_expert, tile_rows, dest = _moe_plan(route, n_exp, bm, n_tiles)
    ys = _moe_experts(hn, token_sorted, gate_sorted, tile_expert, tile_rows, w_gu, w_down, bm=bm, bf=bf,
                      buckets=buckets)
    return [_moe_combine_norm(y, ys, dest, g_out, row0, nrows, bt=bt) for row0, nrows in splits]


def _rope_tables(positions, half, reps):
    inv = ROPE_THETA ** (-np.arange(half, dtype=np.float64) / half)
    ang = np.asarray(positions, np.float64)[:, None] * inv[None, :]
    cos, sin = np.cos(ang), np.sin(ang)
    cos2 = np.tile(np.concatenate([cos, cos], axis=1), (reps, 1))
    sin2 = np.tile(np.concatenate([-sin, sin], axis=1), (reps, 1))
    return jnp.asarray(cos2, F32), jnp.asarray(sin2, F32)


def _step(x_prompt, x_sample, cache_k, cache_v, state_hgrn, page_table, norm_mix_g, norm_ffn_g, norm_out_g,
          hg_w_in, hg_lb_logits, hg_norm_g, hg_w_out, da_w_in, da_lambda, da_subln_g, da_w_out,
          ffn_w_gu, ffn_w_down, moe_w_router, moe_w_gu, moe_w_down, *, cfg):
    nb_p, seq, d = x_prompt.shape
    nb_s, tq, _ = x_sample.shape
    tp, ts = nb_p * seq, nb_s * tq
    t = tp + ts
    past = page_table.shape[1] * cache_k.shape[2]
    dh = cache_k.shape[-1]
    n_sub = cache_k.shape[-2]
    nh_attn = n_sub // 2
    bm, bn, bmp = cfg["bm"], cfg["bn"], cfg["bm_prompt"]
    y = jnp.concatenate([x_prompt.reshape(tp, d), x_sample.reshape(ts, d)], axis=0)

    hk = hg_w_in.shape[2] // 4
    (proj,) = _norm_proj(y, norm_mix_g[0], hg_w_in[0], 0, 4 * hk, 0, t, eps=NORM_EPS, bm=bm, bn=cfg["bn_wide"],
                         out_dtypes=(F32,))
    o_p, st_p = _hgrn_prompt(proj, hg_lb_logits, hg_norm_g[0], nb_p, seq, layer=0, c=cfg["hg_chunk"], hb=cfg["hg_hb"])
    o_s, st_s = _hgrn_sample(proj, tp, state_hgrn[0], hg_lb_logits, hg_norm_g[0], tq, layer=0,
                             nb=cfg["hg_nb"], hb=cfg["hg_hb"])
    y = _matmul_residual(jnp.concatenate([o_p, o_s], axis=0), hg_w_out[0], y, bm=bm, bn=bn)
    y = _ffn_dense(y, norm_ffn_g[0], ffn_w_gu[0], ffn_w_down[0], eps=NORM_EPS, bm=cfg["ffn_bm"], bf=cfg["ffn_bf"])

    lam_init = 0.8 - 0.6 * math.exp(-0.3 * 1)
    qk = n_sub * dh
    g1 = norm_mix_g[1]
    w_in = da_w_in[0]
    cos_p, sin_p = _rope_tables(np.arange(seq), dh // 2, 1)
    cos_s, sin_s = _rope_tables(past + np.arange(tq), dh // 2, nb_s)
    q_scale = dh ** -0.5
    q16_p, k32_p, k16_p, v32_p, v16_p = _norm_qkv(y, g1, w_in, 0, tp, cos_p, sin_p, q_scale, eps=NORM_EPS, bm=bmp, bn=bn)
    q16_s, k32_s, _, v32_s, _ = _norm_qkv(y, g1, w_in, tp, ts, cos_s, sin_s, q_scale, eps=NORM_EPS, bm=ts, bn=bn)
    a_p = _attn_prompt(q16_p, k16_p, v16_p, da_lambda[0], da_subln_g[0], nb_p, seq, lam_init=lam_init,
                       blk=cfg["attn_blk"])
    q_s = q16_s.reshape(nb_s, tq, nh_attn, 2, dh).transpose(0, 2, 3, 1, 4).reshape(nb_s, nh_attn * 2 * tq, dh)
    ck = cache_k[0].reshape(cache_k.shape[1], cache_k.shape[2] * n_sub, dh)
    cv = cache_v[0].reshape(cache_v.shape[1], cache_v.shape[2] * nh_attn, 2 * dh)
    a_s = _attn_sample(q_s, k32_s, v32_s, ck, cv, page_table, da_lambda[0], da_subln_g[0], tq,
                       lam_init=lam_init, pages=cfg["attn_pages"])
    attn = jnp.concatenate([a_p, a_s.reshape(ts, qk).astype(BF16)], axis=0)
    y = _matmul_residual(attn, da_w_out[0], y, bm=bm, bn=bn)
    y_p, y_s = _moe_block_norm(y, norm_ffn_g[1], moe_w_router[0], moe_w_gu[0], moe_w_down[0], norm_out_g,
                               [(0, tp), (tp, ts)], bm_route=cfg["route_bm"], bm=cfg["moe_bm"], bf=cfg["moe_bf"],
                               buckets=cfg["moe_buckets"], bt=cfg["combine_bt"])

    return (y_p.reshape(nb_p, seq, d), y_s.reshape(nb_s, tq, d),
            k32_p.reshape(1, nb_p, seq, n_sub, dh), v32_p.reshape(1, nb_p, seq, nh_attn, 2 * dh),
            k32_s.reshape(1, nb_s, tq, n_sub, dh), v32_s.reshape(1, nb_s, tq, nh_attn, 2 * dh),
            st_p[None], st_s[None])


_CFG = dict(bm=1056, bn=512, bn_wide=1024, bm_prompt=1024, hg_chunk=128, hg_hb=8, hg_nb=8, ffn_bm=704, ffn_bf=512,
            attn_blk=512, attn_pages=8, route_bm=264, moe_bm=1152, moe_bf=512,
            moe_buckets=(256, 512, 768, 1024, 1088, 1152), combine_bt=256)


def kernel(x_prompt, x_sample, cache_k, cache_v, state_hgrn, page_table, norm_mix_g, norm_ffn_g, norm_out_g,
           hg_w_in, hg_lb_logits, hg_norm_g, hg_w_out, da_w_in, da_lambda, da_subln_g, da_w_out,
           ffn_w_gu, ffn_w_down, moe_w_router, moe_w_gu, moe_w_down):
    return _step(x_prompt, x_sample, cache_k, cache_v, state_hgrn, page_table, norm_mix_g, norm_ffn_g, norm_out_g,
                 hg_w_in, hg_lb_logits, hg_norm_g, hg_w_out, da_w_in, da_lambda, da_subln_g, da_w_out,
                 ffn_w_gu, ffn_w_down, moe_w_router, moe_w_gu, moe_w_down, cfg=_CFG)
```

```python
import functools
import math

import numpy as np
import jax
import jax.numpy as jnp
from jax import lax
from jax.experimental import pallas as pl
from jax.experimental.pallas import tpu as pltpu

F32 = jnp.float32
BF16 = jnp.bfloat16

LANES = 128
SUBLANES = 8
NORM_EPS = 1e-6
SUBLN_EPS = 1e-5
ROPE_THETA = 10000.0
NEG_BIG = -1e30
VMEM_LIMIT = 56 * 1024 * 1024


def _params(**kw):
    return pltpu.CompilerParams(vmem_limit_bytes=VMEM_LIMIT, **kw)


def _dot(a, b):
    return jnp.dot(a, b, preferred_element_type=F32)


def _dot_nt(a, b):
    return lax.dot_general(a, b, (((1,), (1,)), ((), ())), preferred_element_type=F32)


def _dot_tn(a, b):
    return lax.dot_general(a, b, (((0,), (0,)), ((), ())), preferred_element_type=F32)


def _sigmoid(x):
    return 1.0 / (1.0 + jnp.exp(-x))


def _silu(x):
    return x * _sigmoid(x)


def _rms(x, g, eps):
    return x * lax.rsqrt(jnp.mean(x * x, axis=-1, keepdims=True) + eps) * g


def _split2(x):
    hi = x.astype(BF16)
    return hi, (x - hi.astype(F32)).astype(BF16)


def _split3(x):
    hi = x.astype(BF16)
    r1 = x - hi.astype(F32)
    mid = r1.astype(BF16)
    lo = (r1 - mid.astype(F32)).astype(BF16)
    return hi, mid, lo


def _norm_proj_kernel(x_ref, g_ref, w_ref, *rest, eps, rope_scale, n_out):
    o_refs, h_ref = rest[-1 - n_out:-1], rest[-1]

    @pl.when(pl.program_id(1) == 0)
    def _():
        h_ref[...] = _rms(x_ref[...], g_ref[...], eps).astype(BF16)

    acc = _dot(h_ref[...], w_ref[...].astype(BF16))
    if rope_scale is None:
        for o_ref in o_refs:
            o_ref[...] = acc.astype(o_ref.dtype)
    else:
        cos = rest[0][...]
        sin = rest[1][...]
        for c in range(acc.shape[1] // LANES):
            sl = slice(c * LANES, (c + 1) * LANES)
            xh = acc[:, sl]
            r = (xh * cos + pltpu.roll(xh, LANES // 2, 1) * sin) * rope_scale
            for o_ref in o_refs:
                o_ref[:, sl] = r.astype(o_ref.dtype)


def _norm_proj(x, g, w, col0, ncols, row0, nrows, *, eps, bm, bn, out_dtypes, rope=None):
    d = x.shape[1]
    cb, rb = col0 // bn, row0 // bm
    in_specs = [
        pl.BlockSpec((bm, d), lambda i, j: (rb + i, 0)),
        pl.BlockSpec((1, d), lambda i, j: (0, 0)),
        pl.BlockSpec((d, bn), lambda i, j: (0, cb + j)),
    ]
    args = [x, g.reshape(1, d), w]
    scale = None
    if rope is not None:
        cos, sin, scale = rope
        period = cos.shape[0] // bm
        in_specs += [pl.BlockSpec((bm, LANES), lambda i, j: (i % period, 0))] * 2
        args += [cos, sin]
    return pl.pallas_call(
        functools.partial(_norm_proj_kernel, eps=eps, rope_scale=scale, n_out=len(out_dtypes)),
        out_shape=tuple(jax.ShapeDtypeStruct((nrows, ncols), dt) for dt in out_dtypes),
        grid=(nrows // bm, ncols // bn),
        in_specs=in_specs,
        out_specs=tuple(pl.BlockSpec((bm, bn), lambda i, j: (i, j)) for _ in out_dtypes),
        scratch_shapes=[pltpu.VMEM((bm, d), BF16)],
        compiler_params=_params(),
    )(*args)


def _norm_qkv_kernel(x_ref, g_ref, w_ref, cos_ref, sin_ref, q16_ref, k32_ref, k16_ref, v32_ref, v16_ref, h_ref,
                     *, eps, q_scale, nq):
    j = pl.program_id(1)

    @pl.when(j == 0)
    def _():
        h_ref[...] = _rms(x_ref[...], g_ref[...], eps).astype(BF16)

    acc = _dot(h_ref[...], w_ref[...].astype(BF16))

    def rotated():
        cos = cos_ref[...]
        sin = sin_ref[...]
        for c in range(acc.shape[1] // LANES):
            sl = slice(c * LANES, (c + 1) * LANES)
            xh = acc[:, sl]
            yield sl, xh * cos + pltpu.roll(xh, LANES // 2, 1) * sin

    @pl.when(j < nq)
    def _():
        for sl, r in rotated():
            q16_ref[:, sl] = (r * q_scale).astype(BF16)

    @pl.when((j >= nq) & (j < 2 * nq))
    def _():
        for sl, r in rotated():
            k32_ref[:, sl] = r
            k16_ref[:, sl] = r.astype(BF16)

    @pl.when(j >= 2 * nq)
    def _():
        v32_ref[...] = acc
        v16_ref[...] = acc.astype(BF16)


def _norm_qkv(x, g, w, row0, nrows, cos, sin, q_scale, *, eps, bm, bn):
    d = x.shape[1]
    width = w.shape[1] // 3
    nq = width // bn
    rb = row0 // bm
    period = cos.shape[0] // bm

    def sect(s):
        return lambda i, j: (i, jnp.clip(j - s * nq, 0, nq - 1))

    def out(dt):
        return jax.ShapeDtypeStruct((nrows, width), dt)

    return pl.pallas_call(
        functools.partial(_norm_qkv_kernel, eps=eps, q_scale=q_scale, nq=nq),
        out_shape=(out(BF16), out(F32), out(BF16), out(F32), out(BF16)),
        grid=(nrows // bm, 3 * nq),
        in_specs=[
            pl.BlockSpec((bm, d), lambda i, j: (rb + i, 0)),
            pl.BlockSpec((1, d), lambda i, j: (0, 0)),
            pl.BlockSpec((d, bn), lambda i, j: (0, j)),
            pl.BlockSpec((bm, LANES), lambda i, j: (i % period, 0)),
            pl.BlockSpec((bm, LANES), lambda i, j: (i % period, 0)),
        ],
        out_specs=(pl.BlockSpec((bm, bn), sect(0)), pl.BlockSpec((bm, bn), sect(1)), pl.BlockSpec((bm, bn), sect(1)),
                   pl.BlockSpec((bm, bn), sect(2)), pl.BlockSpec((bm, bn), sect(2))),
        scratch_shapes=[pltpu.VMEM((bm, d), BF16)],
        compiler_params=_params(),
    )(x, g.reshape(1, d), w, cos, sin)


def _matmul_residual_kernel(a_ref, w_ref, r_ref, o_ref):
    o_ref[...] = r_ref[...] + _dot(a_ref[...], w_ref[...].astype(BF16))


def _matmul_residual(a, w, res, *, bm, bn):
    t, k = a.shape
    n = w.shape[1]
    return pl.pallas_call(
        _matmul_residual_kernel,
        out_shape=jax.ShapeDtypeStruct((t, n), F32),
        grid=(t // bm, n // bn),
        in_specs=[
            pl.BlockSpec((bm, k), lambda i, j: (i, 0)),
            pl.BlockSpec((k, bn), lambda i, j: (0, j)),
            pl.BlockSpec((bm, bn), lambda i, j: (i, j)),
        ],
        out_specs=pl.BlockSpec((bm, bn), lambda i, j: (i, j)),
        compiler_params=_params(),
    )(a, w, res)


def _ffn_kernel(x_ref, g_ref, wg_ref, wu_ref, wd_ref, o_ref, h_ref, *, eps):
    @pl.when(pl.program_id(1) == 0)
    def _():
        x = x_ref[...]
        h_ref[...] = _rms(x, g_ref[...], eps).astype(BF16)
        o_ref[...] = x

    h = h_ref[...]
    gate = _dot(h, wg_ref[...].astype(BF16))
    up = _dot(h, wu_ref[...].astype(BF16))
    act = (_silu(gate) * up).astype(BF16)
    o_ref[...] += _dot(act, wd_ref[...].astype(BF16))


def _ffn_dense(y, g, w_gu, w_down, *, eps, bm, bf):
    t, d = y.shape
    dff = w_down.shape[0]
    nf = dff // bf
    return pl.pallas_call(
        functools.partial(_ffn_kernel, eps=eps),
        out_shape=jax.ShapeDtypeStruct((t, d), F32),
        grid=(t // bm, nf),
        in_specs=[
            pl.BlockSpec((bm, d), lambda i, f: (i, 0)),
            pl.BlockSpec((1, d), lambda i, f: (0, 0)),
            pl.BlockSpec((d, bf), lambda i, f: (0, f)),
            pl.BlockSpec((d, bf), lambda i, f: (0, nf + f)),
            pl.BlockSpec((bf, d), lambda i, f: (f, 0)),
        ],
        out_specs=pl.BlockSpec((bm, d), lambda i, f: (i, 0)),
        scratch_shapes=[pltpu.VMEM((bm, d), BF16)],
        compiler_params=_params(),
    )(y, g.reshape(1, d), w_gu, w_gu, w_down)


def _hgrn_levels(seg):
    levels = []
    m = seg // 2
    while m >= 1:
        levels.append(m)
        m //= 2
    return levels


def _hgrn_consts(c, seg):
    t = np.arange(c)[:, None]
    p = np.arange(c)[None, :]
    same = (t // seg) == (p // seg)
    blocks = [(p <= t) & same]
    masks = []
    for m in _hgrn_levels(seg):
        r = (t // (2 * m)) * (2 * m) + m - 1
        upper = (t % (2 * m)) >= m
        if 2 * m < SUBLANES:
            blocks.append(np.where(upper, (p > r) & (p <= t), (p > t) & (p <= r)))
        masks.append(((t // (2 * m)) == (p // (2 * m))) & upper & ((p % (2 * m)) < m))
    masks.append(t == p)
    w = np.concatenate(blocks, axis=0).astype(np.float32)
    return jnp.asarray(w, BF16), jnp.asarray(np.stack(masks).astype(np.float32))


def _hgrn_lower_bound(logits, layer):
    e = jnp.exp(logits - jnp.max(logits, axis=0, keepdims=True))
    sm = e / jnp.sum(e, axis=0, keepdims=True)
    return jnp.sum(sm[: layer + 1], axis=0, keepdims=True)


def _hgrn_gates(qp, fp, lb, kdim):
    q = _silu(qp) * (kdim ** -0.5)
    f = lb + (1.0 - lb) * _sigmoid(fp)
    return q, jnp.log(f), 1.0 - f


def _block_row(g, blk, off):
    parts = [jnp.broadcast_to(g[b * blk + off:b * blk + off + 1], (blk, g.shape[1])) for b in range(g.shape[0] // blk)]
    return parts[0] if len(parts) == 1 else jnp.concatenate(parts, axis=0)


def _hgrn_intra(q, k, lg, w, m_ref, seg):
    c = q.shape[0]
    levels = _hgrn_levels(seg)
    hi, lo = _split2(lg)
    e = _dot(w, hi) + _dot(w, lo)
    g = e[0:c]
    a = m_ref[len(levels)] * _dot_nt(q.astype(BF16), k.astype(BF16))
    short = 0
    for li, m in enumerate(levels):
        if 2 * m >= SUBLANES:
            xm = jnp.exp(-jnp.abs(g - _block_row(g, 2 * m, m - 1)))
        else:
            short += 1
            xm = jnp.exp(e[short * c:(short + 1) * c])
        a = a + m_ref[li] * _dot_nt((q * xm).astype(BF16), (k * xm).astype(BF16))
    return jnp.exp(g), jnp.exp(-jnp.abs(g - _block_row(g, seg, seg - 1))), a


def _hgrn_prompt_kernel(q_ref, f_ref, i_ref, g_ref, lb_ref, ng_ref, w_ref, m_ref, o_ref, s_ref, st_ref,
                        *, hb, c, layer, eps):
    ci = pl.program_id(2)

    @pl.when(ci == 0)
    def _():
        st_ref[...] = jnp.zeros_like(st_ref)

    w = w_ref[...]
    for hh in range(hb):
        sl = slice(hh * LANES, (hh + 1) * LANES)
        lb = _hgrn_lower_bound(lb_ref[:, sl], layer)
        q, lg, k = _hgrn_gates(q_ref[:, sl], f_ref[:, sl], lb, LANES)
        xq, xk, a = _hgrn_intra(q, k, lg, w, m_ref, c)
        vb = i_ref[:, sl].astype(BF16)
        st = st_ref[hh]
        o = _dot_nt((q * xq).astype(BF16), st.astype(BF16)) + _dot(a.astype(BF16), vb)
        st_ref[hh] = st * xq[c - 1:c] + _dot_tn(vb, (k * xk).astype(BF16))
        o_ref[:, sl] = (_rms(o, ng_ref[...], eps) * _silu(g_ref[:, sl])).astype(o_ref.dtype)

    @pl.when(ci == pl.num_programs(2) - 1)
    def _():
        for hh in range(hb):
            s_ref[0, hh] = st_ref[hh].T


def _hgrn_prompt(proj, lb_logits, norm_g, n_seq, seq_len, *, layer, c, hb):
    hk = proj.shape[1] // 4
    nh = hk // LANES
    nc = seq_len // c
    ngrp = nh // hb
    w, masks = _hgrn_consts(c, c)
    bw = hb * LANES

    def col(part):
        return pl.BlockSpec((c, bw), lambda b, h, ci: (b * nc + ci, part * ngrp + h))

    return pl.pallas_call(
        functools.partial(_hgrn_prompt_kernel, hb=hb, c=c, layer=layer, eps=NORM_EPS),
        out_shape=(jax.ShapeDtypeStruct((n_seq * seq_len, hk), BF16),
                   jax.ShapeDtypeStruct((n_seq, nh, LANES, LANES), F32)),
        grid=(n_seq, ngrp, nc),
        in_specs=[
            col(0), col(1), col(2), col(3),
            pl.BlockSpec((lb_logits.shape[0], bw), lambda b, h, ci: (0, h)),
            pl.BlockSpec((1, LANES), lambda b, h, ci: (0, 0)),
            pl.BlockSpec(w.shape, lambda b, h, ci: (0, 0)),
            pl.BlockSpec(masks.shape, lambda b, h, ci: (0, 0, 0)),
        ],
        out_specs=(pl.BlockSpec((c, bw), lambda b, h, ci: (b * nc + ci, h)),
                   pl.BlockSpec((1, hb, LANES, LANES), lambda b, h, ci: (b, h, 0, 0))),
        scratch_shapes=[pltpu.VMEM((hb, LANES, LANES), F32)],
        compiler_params=_params(),
    )(proj, proj, proj, proj, lb_logits, norm_g.reshape(1, LANES), w, masks)


def _hgrn_sample_kernel(q_ref, f_ref, i_ref, g_ref, lb_ref, ng_ref, w_ref, m_ref, s0_ref, o_ref, s_ref,
                        *, hb, nb, tq, layer, eps):
    c = nb * tq
    w = w_ref[...]
    seq_of_row = lax.broadcasted_iota(jnp.int32, (c, 1), 0) // tq
    for hh in range(hb):
        sl = slice(hh * LANES, (hh + 1) * LANES)
        lb = _hgrn_lower_bound(lb_ref[:, sl], layer)
        q, lg, k = _hgrn_gates(q_ref[:, sl], f_ref[:, sl], lb, LANES)
        xq, xk, a = _hgrn_intra(q, k, lg, w, m_ref, tq)
        vb = i_ref[:, sl].astype(BF16)
        q0 = (q * xq).astype(BF16)
        kl = k * xk
        o = _dot(a.astype(BF16), vb)
        for b in range(nb):
            mine = seq_of_row == b
            st = s0_ref[b, hh].T
            o = o + jnp.where(mine, _dot_nt(q0, st.astype(BF16)), 0.0)
            r_last = (b + 1) * tq - 1
            klb = jnp.where(mine, kl, 0.0).astype(BF16)
            s_ref[b, hh] = (st * xq[r_last:r_last + 1] + _dot_tn(vb, klb)).T
        o_ref[:, sl] = (_rms(o, ng_ref[...], eps) * _silu(g_ref[:, sl])).astype(o_ref.dtype)


def _hgrn_sample(proj, row0, state, lb_logits, norm_g, tq, *, layer, nb, hb):
    n_seq, nh = state.shape[0], state.shape[1]
    hk = proj.shape[1] // 4
    ngrp = nh // hb
    c = nb * tq
    rb0 = row0 // c
    w, masks = _hgrn_consts(c, tq)
    bw = hb * LANES

    def col(part):
        return pl.BlockSpec((c, bw), lambda sb, h: (rb0 + sb, part * ngrp + h))

    return pl.pallas_call(
        functools.partial(_hgrn_sample_kernel, hb=hb, nb=nb, tq=tq, layer=layer, eps=NORM_EPS),
        out_shape=(jax.ShapeDtypeStruct((n_seq * tq, hk), BF16),
                   jax.ShapeDtypeStruct(state.shape, F32)),
        grid=(n_seq // nb, ngrp),
        in_specs=[
            col(0), col(1), col(2), col(3),
            pl.BlockSpec((lb_logits.shape[0], bw), lambda sb, h: (0, h)),
            pl.BlockSpec((1, LANES), lambda sb, h: (0, 0)),
            pl.BlockSpec(w.shape, lambda sb, h: (0, 0)),
            pl.BlockSpec(masks.shape, lambda sb, h: (0, 0, 0)),
            pl.BlockSpec((nb, hb, LANES, LANES), lambda sb, h: (sb, h, 0, 0)),
        ],
        out_specs=(pl.BlockSpec((c, bw), lambda sb, h: (sb, h)),
                   pl.BlockSpec((nb, hb, LANES, LANES), lambda sb, h: (sb, h, 0, 0))),
        compiler_params=_params(),
    )(proj, proj, proj, proj, lb_logits, norm_g.reshape(1, LANES), w, masks, state)


def _diff_lambda(lam_ref, lam_init):
    lp = lam_ref[...]
    s1 = jnp.sum(lp[0:1] * lp[1:2], axis=-1, keepdims=True)
    s2 = jnp.sum(lp[2:3] * lp[3:4], axis=-1, keepdims=True)
    return jnp.exp(s1) - jnp.exp(s2) + lam_init


def _lanes(x, width):
    return x if width == LANES else jnp.concatenate([x] * (width // LANES), axis=1)


def _attn_prompt_kernel(qi_ref, kj_ref, lam_ref, sg_ref, q_ref, k_ref, v_ref, o_ref, m_ref, l_ref, acc_ref,
                        *, blk, lam_init, eps):
    pair = pl.program_id(2)
    i = qi_ref[pair]
    j = kj_ref[pair]
    hv = 2 * LANES

    @pl.when(j == 0)
    def _():
        m_ref[...] = jnp.full_like(m_ref, NEG_BIG)
        l_ref[...] = jnp.zeros_like(l_ref)
        acc_ref[...] = jnp.zeros_like(acc_ref)

    def step(diagonal):
        vb = v_ref[...]
        for s in range(2):
            sl = slice(s * LANES, (s + 1) * LANES)
            sc = _dot_nt(q_ref[:, sl], k_ref[:, sl])
            if diagonal:
                row = lax.broadcasted_iota(jnp.int32, (blk, blk), 0)
                colv = lax.broadcasted_iota(jnp.int32, (blk, blk), 1)
                sc = jnp.where(colv <= row, sc, NEG_BIG)
            m_prev = m_ref[s]
            m_new = jnp.maximum(m_prev, jnp.max(sc, axis=-1, keepdims=True))
            alpha = jnp.exp(m_prev - m_new)
            p = jnp.exp(sc - _lanes(m_new, blk))
            l_ref[s] = alpha * l_ref[s] + jnp.sum(p, axis=-1, keepdims=True)
            acc_ref[s] = _lanes(alpha, hv) * acc_ref[s] + _dot(p.astype(BF16), vb)
            m_ref[s] = m_new

    @pl.when(j < i)
    def _():
        step(False)

    @pl.when(j == i)
    def _():
        step(True)
        lam = _diff_lambda(lam_ref, lam_init)
        o = acc_ref[0] / _lanes(l_ref[0], hv) - lam * (acc_ref[1] / _lanes(l_ref[1], hv))
        o_ref[...] = (_rms(o, sg_ref[...], eps) * (1.0 - lam_init)).astype(o_ref.dtype)


def _attn_prompt(q, k, v, lam_p, subln_g, n_seq, seq_len, *, lam_init, blk):
    hv = 2 * LANES
    nh = q.shape[1] // hv
    nb = seq_len // blk
    pairs = [(i, j) for i in range(nb) for j in range(i + 1)]
    qi = jnp.asarray([p[0] for p in pairs], jnp.int32)
    kj = jnp.asarray([p[1] for p in pairs], jnp.int32)
    return pl.pallas_call(
        functools.partial(_attn_prompt_kernel, blk=blk, lam_init=lam_init, eps=SUBLN_EPS),
        out_shape=jax.ShapeDtypeStruct((n_seq * seq_len, nh * hv), BF16),
        grid_spec=pltpu.PrefetchScalarGridSpec(
            num_scalar_prefetch=2,
            grid=(n_seq, nh, len(pairs)),
            in_specs=[
                pl.BlockSpec(lam_p.shape, lambda b, h, p, qi, kj: (0, 0)),
                pl.BlockSpec((1, hv), lambda b, h, p, qi, kj: (0, 0)),
                pl.BlockSpec((blk, hv), lambda b, h, p, qi, kj: (b * nb + qi[p], h)),
                pl.BlockSpec((blk, hv), lambda b, h, p, qi, kj: (b * nb + kj[p], h)),
                pl.BlockSpec((blk, hv), lambda b, h, p, qi, kj: (b * nb + kj[p], h)),
            ],
            out_specs=pl.BlockSpec((blk, hv), lambda b, h, p, qi, kj: (b * nb + qi[p], h)),
            scratch_shapes=[pltpu.VMEM((2, blk, LANES), F32), pltpu.VMEM((2, blk, LANES), F32),
                            pltpu.VMEM((2, blk, hv), F32)],
        ),
        compiler_params=_params(),
    )(qi, kj, lam_p, subln_g.reshape(1, hv), q, k, v)


def _attn_sample_kernel(pt_ref, lam_ref, sg_ref, q_ref, kn_ref, vn_ref, *rest, pages, tq, nh, lam_init, eps):
    k_refs = rest[:pages]
    v_refs = rest[pages:3 * pages]
    o_ref, m_ref, l_ref, acc_ref = rest[3 * pages:]
    g = pl.program_id(1)
    hv = 2 * LANES
    rows = 2 * tq
    n_sub = 2 * nh

    @pl.when(g == 0)
    def _():
        m_ref[...] = jnp.full_like(m_ref, NEG_BIG)
        l_ref[...] = jnp.zeros_like(l_ref)
        acc_ref[...] = jnp.zeros_like(acc_ref)

    first = lax.broadcasted_iota(jnp.int32, (rows, 1), 0) < tq

    def scores(k_sub):
        out = []
        for h in range(nh):
            qa = q_ref[h * rows:(h + 1) * rows, :]
            out.append(jnp.where(first, _dot_nt(qa, k_sub(2 * h)), _dot_nt(qa, k_sub(2 * h + 1))))
        return jnp.concatenate(out, axis=0)

    def update(sc, v_heads):
        nkeys = sc.shape[1] // len(v_heads)
        m_prev = m_ref[...]
        m_new = jnp.maximum(m_prev, jnp.max(sc, axis=-1, keepdims=True))
        alpha = jnp.exp(m_prev - m_new)
        p = jnp.exp(sc - m_new)
        l_ref[...] = alpha * l_ref[...] + jnp.sum(p, axis=-1, keepdims=True)
        pb = p.astype(BF16)
        pv = []
        for h in range(nh):
            parts = [_dot(pb[h * rows:(h + 1) * rows, b * nkeys:(b + 1) * nkeys], v_head(h))
                     for b, v_head in enumerate(v_heads)]
            pv.append(functools.reduce(lambda x, y: x + y, parts))
        acc_ref[...] = alpha * acc_ref[...] + jnp.concatenate(pv, axis=0)
        m_ref[...] = m_new

    psz = k_refs[0].shape[0] // n_sub

    def k_sub_of(r):
        return lambda j: r[pl.ds(j, psz, stride=n_sub), :].astype(BF16)

    def v_head_of(lo, hi):
        return lambda h: jnp.concatenate([lo[pl.ds(h, psz, stride=nh), :], hi[pl.ds(h, psz, stride=nh), :]],
                                         axis=1).astype(BF16)

    update(jnp.concatenate([scores(k_sub_of(k_refs[pg])) for pg in range(pages)], axis=1),
           [v_head_of(v_refs[2 * pg], v_refs[2 * pg + 1]) for pg in range(pages)])

    @pl.when(g == pl.num_programs(1) - 1)
    def _():
        pad = jnp.zeros((LANES - tq, kn_ref.shape[1]), F32)
        kb = jnp.concatenate([kn_ref[...], pad], axis=0).astype(BF16)
        vb = jnp.concatenate([vn_ref[...], pad], axis=0).astype(BF16)
        sc = scores(lambda j: kb[:, j * LANES:(j + 1) * LANES])
        qpos = lax.broadcasted_iota(jnp.int32, sc.shape, 0) % tq
        kpos = lax.broadcasted_iota(jnp.int32, sc.shape, 1)
        update(jnp.where(kpos <= qpos, sc, NEG_BIG), [lambda h: vb[:, h * hv:(h + 1) * hv]])
        lam = _diff_lambda(lam_ref, lam_init)
        attn = acc_ref[...] / l_ref[...]
        sg = sg_ref[...]
        for h in range(nh):
            o = attn[h * rows:h * rows + tq] - lam * attn[h * rows + tq:(h + 1) * rows]
            o_ref[:, h * hv:(h + 1) * hv] = _rms(o, sg, eps) * (1.0 - lam_init)


def _attn_sample(q16, k_new, v_new, cache_k, cache_v, page_table, lam_p, subln_g, tq, *, lam_init, pages):
    n_seq, n_pages = page_table.shape
    hv = 2 * LANES
    width = k_new.shape[1]
    nh = width // hv
    rows = nh * 2 * tq

    def page_spec(cache, pg, lane_blk):
        return pl.BlockSpec((None, cache.shape[1], LANES),
                            lambda b, g, pt: (pt[b * n_pages + g * pages + pg], 0, lane_blk))

    return pl.pallas_call(
        functools.partial(_attn_sample_kernel, pages=pages, tq=tq, nh=nh, lam_init=lam_init, eps=SUBLN_EPS),
        out_shape=jax.ShapeDtypeStruct((n_seq, tq, width), F32),
        grid_spec=pltpu.PrefetchScalarGridSpec(
            num_scalar_prefetch=1,
            grid=(n_seq, n_pages // pages),
            in_specs=[
                pl.BlockSpec(lam_p.shape, lambda b, g, pt: (0, 0)),
                pl.BlockSpec((1, hv), lambda b, g, pt: (0, 0)),
                pl.BlockSpec((None, rows, LANES), lambda b, g, pt: (b, 0, 0)),
                pl.BlockSpec((tq, width), lambda b, g, pt: (b, 0)),
                pl.BlockSpec((tq, width), lambda b, g, pt: (b, 0)),
            ] + [page_spec(cache_k, pg, 0) for pg in range(pages)]
              + [page_spec(cache_v, pg, half) for pg in range(pages) for half in range(2)],
            out_specs=pl.BlockSpec((None, tq, width), lambda b, g, pt: (b, 0, 0)),
            scratch_shapes=[pltpu.VMEM((rows, 1), F32), pltpu.VMEM((rows, 1), F32), pltpu.VMEM((rows, hv), F32)],
        ),
        compiler_params=_params(),
    )(page_table.reshape(-1), lam_p, subln_g.reshape(1, hv), q16, k_new, v_new,
      *([cache_k] * pages), *([cache_v] * (2 * pages)))


def _moe_route_kernel(y_ref, g_ref, wr_ref, hn_ref, r_ref, *, eps, n_exp):
    h = _rms(y_ref[...], g_ref[...], eps)
    hn_ref[...] = h
    h1, h2, h3 = _split3(h)
    w1, w2, w3 = _split3(wr_ref[...])
    logits = (_dot(h1, w1) + (_dot(h1, w2) + _dot(h2, w1)) + (_dot(h1, w3) + _dot(h2, w2) + _dot(h3, w1)))
    lane = lax.broadcasted_iota(jnp.int32, logits.shape, 1)
    logits = jnp.where(lane < n_exp, logits, NEG_BIG)
    m1 = jnp.max(logits, axis=-1, keepdims=True)
    i1 = jnp.min(jnp.where(logits == m1, lane, LANES), axis=-1, keepdims=True)
    rest = jnp.where(lane == i1, NEG_BIG, logits)
    m2 = jnp.max(rest, axis=-1, keepdims=True)
    i2 = jnp.min(jnp.where(rest == m2, lane, LANES), axis=-1, keepdims=True)
    e2 = jnp.exp(m2 - m1)
    w_top1 = 1.0 / (1.0 + e2)
    w_top2 = e2 / (1.0 + e2)
    r_ref[...] = jnp.where(lane == 0, i1.astype(F32),
                           jnp.where(lane == 1, i2.astype(F32),
                                     jnp.where(lane == 2, w_top1, jnp.where(lane == 3, w_top2, 0.0))))


def _moe_route(y, g, w_router, *, bm):
    t, d = y.shape
    n_exp = w_router.shape[1]
    wr = jnp.pad(w_router, ((0, 0), (0, LANES - n_exp)))
    return pl.pallas_call(
        functools.partial(_moe_route_kernel, eps=NORM_EPS, n_exp=n_exp),
        out_shape=(jax.ShapeDtypeStruct((t, d), F32), jax.ShapeDtypeStruct((t, LANES), F32)),
        grid=(t // bm,),
        in_specs=[
            pl.BlockSpec((bm, d), lambda i: (i, 0)),
            pl.BlockSpec((1, d), lambda i: (0, 0)),
            pl.BlockSpec((d, LANES), lambda i: (0, 0)),
        ],
        out_specs=(pl.BlockSpec((bm, d), lambda i: (i, 0)), pl.BlockSpec((bm, LANES), lambda i: (i, 0))),
        compiler_params=_params(),
    )(y, g.reshape(1, d), wr)


def _row_copy(src_hbm, dst, src_row, dst_row, sem):
    return pltpu.make_async_copy(src_hbm.at[pl.ds(src_row, 1)], dst.at[pl.ds(dst_row, 1)], sem)


def _moe_expert_kernel(te_ref, nr_ref, tok_ref, hn_ref, gate_ref, wg_ref, wu_ref, wd_ref, o_ref, xin_ref, xb_ref, sem,
                       *, bm, buckets):
    i = pl.program_id(0)
    f = pl.program_id(1)
    nrows = nr_ref[i]

    def groups(tile):
        return (nr_ref[tile] + SUBLANES - 1) // SUBLANES

    def issue(tile):
        def body(grp, carry):
            for u in range(SUBLANES):
                r = grp * SUBLANES + u
                _row_copy(hn_ref, xin_ref, tok_ref[tile * bm + r], r, sem).start()
            return carry
        lax.fori_loop(0, groups(tile), body, 0)

    @pl.when(f == 0)
    def _():
        @pl.when(i == 0)
        def _():
            xin_ref[...] = jnp.zeros_like(xin_ref)
            issue(0)

        def drain_rows(n):
            def body(c, carry):
                pltpu.make_async_copy(hn_ref.at[pl.ds(0, n)], xin_ref.at[pl.ds(0, n)], sem).wait()
                return carry
            return body
        ngrp = groups(i)
        lax.fori_loop(0, ngrp // SUBLANES, drain_rows(SUBLANES * SUBLANES), 0)
        lax.fori_loop(0, ngrp % SUBLANES, drain_rows(SUBLANES), 0)
        row = lax.broadcasted_iota(jnp.int32, (bm, 1), 0)
        xb_ref[...] = jnp.where(row < nrows, xin_ref[...], 0.0).astype(BF16)
        o_ref[...] = jnp.zeros_like(o_ref)

        @pl.when(i + 1 < pl.num_programs(0))
        def _():
            issue(i + 1)

    for lo, m_rows in zip((0,) + tuple(buckets[:-1]), buckets):
        @pl.when((nrows > lo) & (nrows <= m_rows))
        def _(m_rows=m_rows):
            x = xb_ref[0:m_rows]
            gate = _dot(x, wg_ref[...].astype(BF16))
            up = _dot(x, wu_ref[...].astype(BF16))
            act = (_silu(gate) * up).astype(BF16)
            o_ref[0:m_rows] += _dot(act, wd_ref[...].astype(BF16))

    @pl.when(f == pl.num_programs(1) - 1)
    def _():
        o_ref[...] = o_ref[...] * gate_ref[...]


def _moe_experts(hn, token_sorted, gate_sorted, tile_expert, tile_rows, w_gu, w_down, *, bm, bf, buckets):
    d = hn.shape[1]
    n_tiles = tile_expert.shape[0]
    dff = w_down.shape[1]
    nf = dff // bf

    def fblk(i, f, nr):
        return jnp.where(nr[i] > 0, f, nf - 1)

    return pl.pallas_call(
        functools.partial(_moe_expert_kernel, bm=bm, buckets=buckets),
        out_shape=jax.ShapeDtypeStruct((n_tiles * bm, d), F32),
        grid_spec=pltpu.PrefetchScalarGridSpec(
            num_scalar_prefetch=3,
            grid=(n_tiles, nf),
            in_specs=[
                pl.BlockSpec(memory_space=pl.ANY),
                pl.BlockSpec((bm, 1), lambda i, f, te, nr, tok: (i, 0)),
                pl.BlockSpec((None, d, bf), lambda i, f, te, nr, tok: (te[i], 0, fblk(i, f, nr))),
                pl.BlockSpec((None, d, bf), lambda i, f, te, nr, tok: (te[i], 0, nf + fblk(i, f, nr))),
                pl.BlockSpec((None, bf, d), lambda i, f, te, nr, tok: (te[i], fblk(i, f, nr), 0)),
            ],
            out_specs=pl.BlockSpec((bm, d), lambda i, f, te, nr, tok: (i, 0), pipeline_mode=pl.Buffered(1)),
            scratch_shapes=[pltpu.VMEM((bm, d), F32), pltpu.VMEM((bm, d), BF16), pltpu.SemaphoreType.DMA],
        ),
        compiler_params=_params(),
    )(tile_expert, tile_rows, token_sorted, hn, gate_sorted.reshape(n_tiles * bm, 1), w_gu, w_gu, w_down)


def _moe_combine_kernel(pos_ref, y_ref, g_ref, ys_ref, o_ref, buf_ref, sem, *, bt, eps):
    def issue(r, carry):
        _row_copy(ys_ref, buf_ref.at[0], pos_ref[0, 0, 2 * r], r, sem).start()
        _row_copy(ys_ref, buf_ref.at[1], pos_ref[0, 0, 2 * r + 1], r, sem).start()
        return carry

    lax.fori_loop(0, bt, issue, 0, unroll=SUBLANES)

    for half in range(2):
        pltpu.make_async_copy(ys_ref.at[pl.ds(0, bt)], buf_ref.at[half], sem).wait()
    z = y_ref[...] + (buf_ref[0] + buf_ref[1])
    o_ref[...] = _rms(z, g_ref[...], eps)


def _moe_combine_norm(y, ys, pos, g, row0, nrows, *, bt):
    d = y.shape[1]
    steps = nrows // bt
    rb = row0 // bt
    return pl.pallas_call(
        functools.partial(_moe_combine_kernel, bt=bt, eps=NORM_EPS),
        out_shape=jax.ShapeDtypeStruct((nrows, d), F32),
        grid=(steps,),
        in_specs=[
            pl.BlockSpec((1, 1, 2 * bt), lambda i: (i, 0, 0), memory_space=pltpu.SMEM),
            pl.BlockSpec((bt, d), lambda i: (rb + i, 0)),
            pl.BlockSpec((1, d), lambda i: (0, 0)),
            pl.BlockSpec(memory_space=pl.ANY),
        ],
        out_specs=pl.BlockSpec((bt, d), lambda i: (i, 0)),
        scratch_shapes=[pltpu.VMEM((2, bt, d), F32), pltpu.SemaphoreType.DMA],
        compiler_params=_params(),
    )(pos[2 * row0:2 * (row0 + nrows)].reshape(steps, 1, 2 * bt), y, g.reshape(1, d), ys)


def _exclusive_cumsum_rows(x):
    n, k = x.shape
    nblk = n // LANES
    xb = x.reshape(nblk, LANES, k).astype(F32)
    strict = jnp.asarray(np.tril(np.ones((LANES, LANES), np.float32), -1))
    within = jnp.einsum("ij,bjk->bik", strict, xb, precision=lax.Precision.HIGHEST)
    totals = jnp.sum(xb, axis=1)
    before = jnp.cumsum(totals, axis=0) - totals
    return (within + before[:, None, :]).astype(jnp.int32).reshape(n, k)


def _moe_plan(route, n_exp, bm, n_tiles):
    t = route.shape[0]
    experts = route[:, 0:2].astype(jnp.int32).reshape(-1)
    gates = route[:, 2:4].reshape(-1)
    onehot = (experts[:, None] == jnp.arange(n_exp, dtype=jnp.int32)[None, :]).astype(jnp.int32)
    rank = jnp.sum(_exclusive_cumsum_rows(onehot) * onehot, axis=1)
    counts = jnp.sum(onehot, axis=0)
    tiles_per = (counts + bm - 1) // bm
    rows_per = (((counts + jnp.maximum(tiles_per, 1) - 1) // jnp.maximum(tiles_per, 1)) + 7) // 8 * 8
    rows_per = jnp.maximum(rows_per, 8)
    tile_end = jnp.cumsum(tiles_per)
    tile_start = tile_end - tiles_per
    dest = (tile_start[experts] + rank // rows_per[experts]) * bm + rank % rows_per[experts]
    slot_vals = jnp.stack([(jnp.arange(2 * t, dtype=jnp.int32) // 2).astype(F32), gates], axis=1)
    sorted_vals = jnp.zeros((n_tiles * bm, 2), F32).at[dest].set(slot_vals)
    token_sorted = sorted_vals[:, 0].astype(jnp.int32)
    gate_sorted = sorted_vals[:, 1]
    n_used = tile_end[-1]
    tiles = jnp.arange(n_tiles, dtype=jnp.int32)
    tile_expert = jnp.sum((jnp.minimum(tiles, n_used - 1)[:, None] >= tile_end[None, :]).astype(jnp.int32), axis=1)
    local = tiles - tile_start[tile_expert]
    tile_rows = jnp.clip(counts[tile_expert] - local * rows_per[tile_expert], 0, rows_per[tile_expert])
    tile_rows = jnp.where(tiles < n_used, tile_rows, 0)
    return token_sorted, gate_sorted, tile_expert.astype(jnp.int32), tile_rows.astype(jnp.int32), dest


def _moe_block_norm(y, g_ffn, w_router, w_gu, w_down, g_out, splits, *, bm_route, bm, bf, buckets, bt):
    t = y.shape[0]
    n_exp = w_router.shape[1]
    hn, route = _moe_route(y, g_ffn, w_router, bm=bm_route)
    n_tiles = (2 * t) // bm + n_exp
    token_sorted, gate_sorted, tile_expert, tile_rows, dest = _moe_plan(route, n_exp, bm, n_tiles)
    ys = _moe_experts(hn, token_sorted, gate_sorted, tile_expert, tile_rows, w_gu, w_down, bm=bm, bf=bf,
                      buckets=buckets)
    return [_moe_combine_norm(y, ys, dest, g_out, row0, nrows, bt=bt) for row0, nrows in splits]


def _rope_tables(positions, half, reps):
    inv = ROPE_THETA ** (-np.arange(half, dtype=np.float64) / half)
    ang = np.asarray(positions, np.float64)[:, None] * inv[None, :]
    cos, sin = np.cos(ang), np.sin(ang)
    cos2 = np.tile(np.concatenate([cos, cos], axis=1), (reps, 1))
    sin2 = np.tile(np.concatenate([-sin, sin], axis=1), (reps, 1))
    return jnp.asarray(cos2, F32), jnp.asarray(sin2, F32)


def _step(x_prompt, x_sample, cache_k, cache_v, state_hgrn, page_table, norm_mix_g, norm_ffn_g, norm_out_g,
          hg_w_in, hg_lb_logits, hg_norm_g, hg_w_out, da_w_in, da_lambda, da_subln_g, da_w_out,
          ffn_w_gu, ffn_w_down, moe_w_router, moe_w_gu, moe_w_down, *, cfg):
    nb_p, seq, d = x_prompt.shape
    nb_s, tq, _ = x_sample.shape
    tp, ts = nb_p * seq, nb_s * tq
    t = tp + ts
    past = page_table.shape[1] * cache_k.shape[2]
    dh = cache_k.shape[-1]
    n_sub = cache_k.shape[-2]
    nh_attn = n_sub // 2
    bm, bn, bmp = cfg["bm"], cfg["bn"], cfg["bm_prompt"]
    y = jnp.concatenate([x_prompt.reshape(tp, d), x_sample.reshape(ts, d)], axis=0)

    hk = hg_w_in.shape[2] // 4
    (proj,) = _norm_proj(y, norm_mix_g[0], hg_w_in[0], 0, 4 * hk, 0, t, eps=NORM_EPS, bm=bm, bn=cfg["bn_wide"],
                         out_dtypes=(F32,))
    o_p, st_p = _hgrn_prompt(proj, hg_lb_logits, hg_norm_g[0], nb_p, seq, layer=0, c=cfg["hg_chunk"], hb=cfg["hg_hb"])
    o_s, st_s = _hgrn_sample(proj, tp, state_hgrn[0], hg_lb_logits, hg_norm_g[0], tq, layer=0,
                             nb=cfg["hg_nb"], hb=cfg["hg_hb"])
    y = _matmul_residual(jnp.concatenate([o_p, o_s], axis=0), hg_w_out[0], y, bm=bm, bn=bn)
    y = _ffn_dense(y, norm_ffn_g[0], ffn_w_gu[0], ffn_w_down[0], eps=NORM_EPS, bm=cfg["ffn_bm"], bf=cfg["ffn_bf"])

    lam_init = 0.8 - 0.6 * math.exp(-0.3 * 1)
    qk = n_sub * dh
    g1 = norm_mix_g[1]
    w_in = da_w_in[0]
    cos_p, sin_p = _rope_tables(np.arange(seq), dh // 2, 1)
    cos_s, sin_s = _rope_tables(past + np.arange(tq), dh // 2, nb_s)
    q_scale = dh ** -0.5
    q16_p, k32_p, k16_p, v32_p, v16_p = _norm_qkv(y, g1, w_in, 0, tp, cos_p, sin_p, q_scale, eps=NORM_EPS, bm=bmp, bn=bn)
    q16_s, k32_s, _, v32_s, _ = _norm_qkv(y, g1, w_in, tp, ts, cos_s, sin_s, q_scale, eps=NORM_EPS, bm=ts, bn=bn)
    a_p = _attn_prompt(q16_p, k16_p, v16_p, da_lambda[0], da_subln_g[0], nb_p, seq, lam_init=lam_init,
                       blk=cfg["attn_blk"])
    q_s = q16_s.reshape(nb_s, tq, nh_attn, 2, dh).transpose(0, 2, 3, 1, 4).reshape(nb_s, nh_attn * 2 * tq, dh)
    ck = cache_k[0].reshape(cache_k.shape[1], cache_k.shape[2] * n_sub, dh)
    cv = cache_v[0].reshape(cache_v.shape[1], cache_v.shape[2] * nh_attn, 2 * dh)
    a_s = _attn_sample(q_s, k32_s, v32_s, ck, cv, page_table, da_lambda[0], da_subln_g[0], tq,
                       lam_init=lam_init, pages=cfg["attn_pages"])
    attn = jnp.concatenate([a_p, a_s.reshape(ts, qk).astype(BF16)], axis=0)
    y = _matmul_residual(attn, da_w_out[0], y, bm=bm, bn=bn)
    y_p, y_s = _moe_block_norm(y, norm_ffn_g[1], moe_w_router[0], moe_w_gu[0], moe_w_down[0], norm_out_g,
                               [(0, tp), (tp, ts)], bm_route=cfg["route_bm"], bm=cfg["moe_bm"], bf=cfg["moe_bf"],
                               buckets=cfg["moe_buckets"], bt=cfg["combine_bt"])

    return (y_p.reshape(nb_p, seq, d), y_s.reshape(nb_s, tq, d),
            k32_p.reshape(1, nb_p, seq, n_sub, dh), v32_p.reshape(1, nb_p, seq, nh_attn, 2 * dh),
            k32_s.reshape(1, nb_s, tq, n_sub, dh), v32_s.reshape(1, nb_s, tq, nh_attn, 2 * dh),
            st_p[None], st_s[None])


_CFG = dict(bm=1056, bn=512, bn_wide=1024, bm_prompt=1024, hg_chunk=128, hg_hb=8, hg_nb=8, ffn_bm=704, ffn_bf=512,
            attn_blk=1024, attn_pages=8, route_bm=264, moe_bm=1152, moe_bf=512,
            moe_buckets=(256, 512, 768, 1024, 1088, 1152), combine_bt=256)


def kernel(x_prompt, x_sample, cache_k, cache_v, state_hgrn, page_table, norm_mix_g, norm_ffn_g, norm_out_g,
           hg_w_in, hg_lb_logits, hg_norm_g, hg_w_out, da_w_in, da_lambda, da_subln_g, da_w_out,
           ffn_w_gu, ffn_w_down, moe_w_router, moe_w_gu, moe_w_down):
    return _step(x_prompt, x_sample, cache_k, cache_v, state_hgrn, page_table, norm_mix_g, norm_ffn_g, norm_out_g,
                 hg_w_in, hg_lb_logits, hg_norm_g, hg_w_out, da_w_in, da_lambda, da_subln_g, da_w_out,
                 ffn_w_gu, ffn_w_down, moe_w_router, moe_w_gu, moe_w_down, cfg=_CFG)
```
